```python
import jax, jax.numpy as jnp
from jax import lax
import numpy as np

D_MODEL = 1024
BATCH = 8
SEQ = 2048
DEPTH = 4
DEC_BATCH = 128
DEC_SEQ = 1
PAST_LEN = 16384
PAGE_SIZE = 128

N_MIXERS = 4
D_PLE = 256
D_FF = 4 * D_MODEL
LN_EPS = 1e-5
DEEPNORM_ALPHA = (2 * DEPTH) ** 0.25
DEEPNORM_BETA = (8 * DEPTH) ** -0.25
CHUNK = 128
D_A = D_MODEL
A_GROUP = 128
A_HEADS = D_A // A_GROUP
POOL_WINDOWS = (2, 4, 8, 16)
POOL_GROUPS = len(POOL_WINDOWS)
D_B = D_MODEL
B_GROUP = D_B // POOL_GROUPS
POOL_BUF = max(POOL_WINDOWS) - 1
D_C = D_MODEL
CONV_K = 31
D_D = D_MODEL
SCONV_K = 3

kernel_name = 'hybrid_chunkmlp_pool_conformer_shortconv_decoder_step'


def layer_norm(x, g, b):
    xf = x.astype(jnp.float32)
    mu = jnp.mean(xf, axis=-1, keepdims=True)
    xc = xf - mu
    var = jnp.mean(xc * xc, axis=-1, keepdims=True)
    y = xc * lax.rsqrt(var + LN_EPS) * g.astype(jnp.float32) + b.astype(jnp.float32)
    return y.astype(x.dtype)


def causal_dwconv(buf, x, w):
    k = w.shape[0]
    ext = jnp.concatenate([buf.astype(x.dtype), x], axis=1)
    y = lax.conv_general_dilated(ext, w.astype(x.dtype)[:, None, :], window_strides=(1,), padding='VALID',
                                 dimension_numbers=('NWC', 'WIO', 'NWC'), feature_group_count=x.shape[-1])
    return y, ext[:, ext.shape[1] - (k - 1):]


def mixer_chunk_mlp(x, w_in, b_in, ln_g, ln_b, w_s, b_s, w_out):
    z = jax.nn.gelu(x @ w_in + b_in)
    u, v = jnp.split(z, 2, axis=-1)
    vn = layer_norm(v, ln_g, ln_b)
    bsz, length, _ = vn.shape
    n_chunks = -(-length // CHUNK)
    pad = n_chunks * CHUNK - length
    vp = jnp.pad(vn, ((0, 0), (0, pad), (0, 0))).reshape(bsz, n_chunks, CHUNK, A_HEADS, A_GROUP)
    mask = jnp.tril(jnp.ones((CHUNK, CHUNK), dtype=bool))
    ws = jnp.where(mask[None], w_s, jnp.zeros_like(w_s))
    s = jnp.einsum('hij,bcjhd->bcihd', ws, vp) + b_s.T[None, None, :, :, None]
    s = s.reshape(bsz, n_chunks * CHUNK, D_A)[:, :length]
    out = (u * s) @ w_out
    last_start = ((length - 1) // CHUNK) * CHUNK
    return out, vn[:, last_start:]


def mixer_pool(x, buf, pos0, w_in, w_grp, scale, w_out):
    h = x @ w_in
    bsz, length, _ = h.shape
    ext = jnp.concatenate([buf.astype(h.dtype), h], axis=1)
    cs = jnp.pad(jnp.cumsum(ext.astype(jnp.float32), axis=1), ((0, 0), (1, 0), (0, 0)))
    pos = pos0 + jnp.arange(length)
    pieces = []
    for g, w in enumerate(POOL_WINDOWS):
        sl = slice(g * B_GROUP, (g + 1) * B_GROUP)
        end = cs[:, POOL_BUF + 1:POOL_BUF + 1 + length, sl]
        start = cs[:, POOL_BUF + 1 - w:POOL_BUF + 1 - w + length, sl]
        cnt = jnp.minimum(w, pos + 1).astype(jnp.float32)[None, :, None]
        pieces.append((end - start) / cnt)
    pooled = jnp.concatenate(pieces, axis=-1).astype(h.dtype) - h
    mixed = jnp.einsum('blgc,gcd->blgd', pooled.reshape(bsz, length, POOL_GROUPS, B_GROUP), w_grp)
    out = (mixed.reshape(bsz, length, D_B) * scale) @ w_out
    return out, ext[:, ext.shape[1] - POOL_BUF:]


def mixer_conformer_conv(x, buf, w_in, b_in, w_dw, b_dw, ln_g, ln_b, w_out):
    a, gate = jnp.split(x @ w_in + b_in, 2, axis=-1)
    g = a * jax.nn.sigmoid(gate)
    y, new_buf = causal_dwconv(buf, g, w_dw)
    y = jax.nn.silu(layer_norm(y + b_dw, ln_g, ln_b))
    return y @ w_out, new_buf


def mixer_short_conv(x, buf, w_in, w_conv, w_out):
    bg, cg, h = jnp.split(x @ w_in, 3, axis=-1)
    y, new_buf = causal_dwconv(buf, cg * h, w_conv)
    return (bg * y) @ w_out, new_buf


def trunk(x, p, pool_buf, conv_buf, sconv_buf, pos0, prm):
    chunk_v = pool_new = conv_new = sconv_new = None
    for i in range(DEPTH):
        kind = i % N_MIXERS
        if kind == 0:
            mix, chunk_v = mixer_chunk_mlp(x, prm['a_w_in'], prm['a_b_in'], prm['a_ln_g'], prm['a_ln_b'],
                                           prm['a_w_s'], prm['a_b_s'], prm['a_w_out'])
        elif kind == 1:
            mix, pool_new = mixer_pool(x, pool_buf, pos0, prm['b_w_in'], prm['b_w_grp'], prm['b_scale'], prm['b_w_out'])
        elif kind == 2:
            mix, conv_new = mixer_conformer_conv(x, conv_buf, prm['c_w_in'], prm['c_b_in'], prm['c_w_dw'], prm['c_b_dw'],
                                                 prm['c_ln_g'], prm['c_ln_b'], prm['c_w_out'])
        else:
            mix, sconv_new = mixer_short_conv(x, sconv_buf, prm['d_w_in'], prm['d_w_conv'], prm['d_w_out'])
        x = layer_norm(DEEPNORM_ALPHA * x + mix, prm['ln1_g'][i], prm['ln1_b'][i])
        hid = jnp.square(jax.nn.relu(x @ prm['mlp_w_up'][i]))
        x = layer_norm(DEEPNORM_ALPHA * x + hid @ prm['mlp_w_down'][i], prm['ln2_g'][i], prm['ln2_b'][i])
        x = x + (p[i] @ prm['ple_w_proj'][i]) * jax.nn.sigmoid(x @ prm['ple_w_gate'][i])
    return x, chunk_v, pool_new, conv_new, sconv_new


def setup_inputs(seed: int = 0) -> dict:
    key = jax.random.key(seed)
    ks = iter(jax.random.split(key, 48))
    f32 = jnp.float32

    def nrm(shape, scale):
        return jax.random.normal(next(ks), shape, f32) * scale

    def gain(shape):
        return 1.0 + nrm(shape, 0.05)

    beta = DEEPNORM_BETA
    d = D_MODEL
    return {
        'x_prompt': nrm((BATCH, SEQ, d), 1.0),
        'x_sample': nrm((DEC_BATCH, DEC_SEQ, d), 1.0),
        'state_pool': nrm((DEC_BATCH, POOL_BUF, D_B), 1.0),
        'state_conv': nrm((DEC_BATCH, CONV_K - 1, D_C), 0.5),
        'state_shortconv': nrm((DEC_BATCH, SCONV_K - 1, D_D), 1.0),
        'p_prompt': nrm((DEPTH, BATCH, SEQ, D_PLE), 1.0),
        'p_sample': nrm((DEPTH, DEC_BATCH, DEC_SEQ, D_PLE), 1.0),
        'a_w_in': nrm((d, 2 * D_A), d ** -0.5),
        'a_b_in': nrm((2 * D_A,), 0.02),
        'a_ln_g': gain((D_A,)),
        'a_ln_b': nrm((D_A,), 0.02),
        'a_w_s': nrm((A_HEADS, CHUNK, CHUNK), CHUNK ** -0.5),
        'a_b_s': gain((A_HEADS, CHUNK)),
        'a_w_out': nrm((D_A, d), D_A ** -0.5 * beta),
        'b_w_in': nrm((d, D_B), d ** -0.5),
        'b_w_grp': nrm((POOL_GROUPS, B_GROUP, B_GROUP), B_GROUP ** -0.5),
        'b_scale': gain((D_B,)),
        'b_w_out': nrm((D_B, d), D_B ** -0.5 * beta),
        'c_w_in': nrm((d, 2 * D_C), d ** -0.5),
        'c_b_in': nrm((2 * D_C,), 0.02),
        'c_w_dw': nrm((CONV_K, D_C), CONV_K ** -0.5),
        'c_b_dw': nrm((D_C,), 0.02),
        'c_ln_g': gain((D_C,)),
        'c_ln_b': nrm((D_C,), 0.02),
        'c_w_out': nrm((D_C, d), D_C ** -0.5 * beta),
        'd_w_in': nrm((d, 3 * D_D), d ** -0.5),
        'd_w_conv': nrm((SCONV_K, D_D), SCONV_K ** -0.5),
        'd_w_out': nrm((D_D, d), D_D ** -0.5 * beta),
        'ln1_g': gain((DEPTH, d)),
        'ln1_b': nrm((DEPTH, d), 0.02),
        'ln2_g': gain((DEPTH, d)),
        'ln2_b': nrm((DEPTH, d), 0.02),
        'mlp_w_up': nrm((DEPTH, d, D_FF), d ** -0.5),
        'mlp_w_down': nrm((DEPTH, D_FF, d), D_FF ** -0.5 * beta),
        'ple_w_proj': nrm((DEPTH, D_PLE, d), D_PLE ** -0.5),
        'ple_w_gate': nrm((DEPTH, d, d), d ** -0.5),
    }


def reference(x_prompt, x_sample, state_pool, state_conv, state_shortconv, p_prompt, p_sample,
              a_w_in, a_b_in, a_ln_g, a_ln_b, a_w_s, a_b_s, a_w_out,
              b_w_in, b_w_grp, b_scale, b_w_out,
              c_w_in, c_b_in, c_w_dw, c_b_dw, c_ln_g, c_ln_b, c_w_out,
              d_w_in, d_w_conv, d_w_out,
              ln1_g, ln1_b, ln2_g, ln2_b, mlp_w_up, mlp_w_down, ple_w_proj, ple_w_gate):
    prm = {
        'a_w_in': a_w_in, 'a_b_in': a_b_in, 'a_ln_g': a_ln_g, 'a_ln_b': a_ln_b,
        'a_w_s': a_w_s, 'a_b_s': a_b_s, 'a_w_out': a_w_out,
        'b_w_in': b_w_in, 'b_w_grp': b_w_grp, 'b_scale': b_scale, 'b_w_out': b_w_out,
        'c_w_in': c_w_in, 'c_b_in': c_b_in, 'c_w_dw': c_w_dw, 'c_b_dw': c_b_dw,
        'c_ln_g': c_ln_g, 'c_ln_b': c_ln_b, 'c_w_out': c_w_out,
        'd_w_in': d_w_in, 'd_w_conv': d_w_conv, 'd_w_out': d_w_out,
        'ln1_g': ln1_g, 'ln1_b': ln1_b, 'ln2_g': ln2_g, 'ln2_b': ln2_b,
        'mlp_w_up': mlp_w_up, 'mlp_w_down': mlp_w_down,
        'ple_w_proj': ple_w_proj, 'ple_w_gate': ple_w_gate,
    }
    dt = x_prompt.dtype
    zero_pool = jnp.zeros((BATCH, POOL_BUF, D_B), dt)
    zero_conv = jnp.zeros((BATCH, CONV_K - 1, D_C), dt)
    zero_sconv = jnp.zeros((BATCH, SCONV_K - 1, D_D), dt)
    y_prompt, chunkv_prompt, pool_prompt, conv_prompt, sconv_prompt = trunk(
        x_prompt, p_prompt, zero_pool, zero_conv, zero_sconv, 0, prm)
    y_sample, chunkv_sample, pool_sample, conv_sample, sconv_sample = trunk(
        x_sample, p_sample, state_pool, state_conv, state_shortconv, PAST_LEN, prm)
    return (y_prompt, y_sample, chunkv_prompt, chunkv_sample, pool_prompt, pool_sample,
            conv_prompt, conv_sample, sconv_prompt, sconv_sample)
```

```python
import functools

import jax
import jax.numpy as jnp
from jax import lax
from jax.experimental import pallas as pl
from jax.experimental.pallas import tpu as pltpu

D_MODEL = 1024
D_PLE = 256
D_FF = 4 * D_MODEL
DEPTH = 4
PAST_LEN = 16384
LN_EPS = 1e-5
DEEPNORM_ALPHA = (2 * DEPTH) ** 0.25
CHUNK = 128
A_GROUP = 128
A_HEADS = D_MODEL // A_GROUP
POOL_WINDOWS = (2, 4, 8, 16)
B_GROUP = D_MODEL // len(POOL_WINDOWS)
POOL_BUF = max(POOL_WINDOWS) - 1
CONV_K = 31
SCONV_K = 3

SUBLANES = 8
V7X_VMEM_BYTES = 64 * 1024 * 1024
VMEM_LIMIT_BYTES = V7X_VMEM_BYTES - 8 * 1024 * 1024

PROMPT_TILE = 256
POOL_HALO = 16
CONV_HALO = 32
SCONV_HALO = 8

F32 = jnp.float32
BF16 = jnp.bfloat16


def _mm(a, w):
    return jnp.dot(a.astype(BF16), w, preferred_element_type=F32)


def _layer_norm(x, g, b):
    mu = jnp.mean(x, axis=-1, keepdims=True)
    xc = x - mu
    var = jnp.mean(xc * xc, axis=-1, keepdims=True)
    return xc * lax.rsqrt(var + LN_EPS) * g + b


def _dense_tail(x, mix, p, ln1_g, ln1_b, w_up, w_down, ln2_g, ln2_b, w_proj, w_gate):
    x1 = _layer_norm(DEEPNORM_ALPHA * x + mix, ln1_g[...], ln1_b[...])
    hid = jnp.square(jnp.maximum(_mm(x1, w_up[...]), 0.0))
    x2 = _layer_norm(DEEPNORM_ALPHA * x1 + _mm(hid, w_down[...]), ln2_g[...], ln2_b[...])
    return x2 + _mm(p, w_proj[...]) * jax.nn.sigmoid(_mm(x2, w_gate[...]))


def _gmlp_front(x, w_in, b_in, ln_g, ln_b):
    z = jax.nn.gelu(_mm(x, w_in[...]) + b_in[...], approximate=True)
    u = z[:, :D_MODEL]
    vn = _layer_norm(z[:, D_MODEL:], ln_g[...], ln_b[...])
    return u, vn


def _glu_front(x, w_in, b_in):
    z = _mm(x, w_in[...]) + b_in[...]
    return z[:, :D_MODEL] * jax.nn.sigmoid(z[:, D_MODEL:])


def _conformer_back(y, b_dw, ln_g, ln_b, w_out):
    yn = _layer_norm(y + b_dw[...], ln_g[...], ln_b[...])
    return _mm(yn * jax.nn.sigmoid(yn), w_out[...])


def _pool_back(pooled, h, w_grp, scale, w_out):
    d = pooled - h
    mixed = [_mm(d[:, g * B_GROUP:(g + 1) * B_GROUP], w_grp[g]) for g in range(len(POOL_WINDOWS))]
    return _mm(jnp.concatenate(mixed, axis=1) * scale[...], w_out[...])


def _prompt_gmlp_kernel(x_ref, p_ref, w_in, b_in, ln_g, ln_b, w_s, b_s, w_out, *rest):
    dense, (y_ref, chunkv_ref) = rest[:8], rest[8:]
    tile = x_ref.shape[0]
    x = x_ref[...]
    u, vn = _gmlp_front(x, w_in, b_in, ln_g, ln_b)
    vnb = vn.astype(BF16)
    row = lax.broadcasted_iota(jnp.int32, (CHUNK, CHUNK), 0)
    col = lax.broadcasted_iota(jnp.int32, (CHUNK, CHUNK), 1)
    causal = [jnp.where(row >= col, w_s[h], 0.0).astype(BF16) for h in range(A_HEADS)]
    chunks = []
    for c in range(tile // CHUNK):
        heads = []
        for h in range(A_HEADS):
            blk = vnb[c * CHUNK:(c + 1) * CHUNK, h * A_GROUP:(h + 1) * A_GROUP]
            heads.append(jnp.dot(causal[h], blk, preferred_element_type=F32) + b_s[h])
        chunks.append(jnp.concatenate(heads, axis=1))
    s = jnp.concatenate(chunks, axis=0)
    mix = _mm(u * s, w_out[...])

    @pl.when(pl.program_id(1) == pl.num_programs(1) - 1)
    def _():
        chunkv_ref[...] = vn[tile - CHUNK:, :]

    y_ref[...] = _dense_tail(x, mix, p_ref[...], *dense)


def _tile_positions(tile):
    return pl.program_id(1) * tile + lax.broadcasted_iota(jnp.int32, (tile, 1), 0)


def _prompt_pool_kernel(x_ref, p_ref, w_in, w_grp, scale, w_out, *rest):
    dense, (y_ref, pool_ref, halo) = rest[:8], rest[8:]
    tile = x_ref.shape[0]
    x = x_ref[...]
    h = _mm(x, w_in[...])

    @pl.when(pl.program_id(1) == 0)
    def _():
        halo[...] = jnp.zeros_like(halo)

    ext = jnp.concatenate([halo[...], h], axis=0)
    pos1 = (_tile_positions(tile) + 1).astype(F32)
    pooled = []
    acc = ext
    for g, w in enumerate(POOL_WINDOWS):
        acc = acc[:, B_GROUP * min(g, 1):]
        acc = acc + pltpu.roll(acc, w // 2, axis=0)
        cnt = jnp.minimum(float(w), pos1)
        pooled.append(acc[POOL_HALO:, :B_GROUP] / cnt)
    halo[...] = h[tile - POOL_HALO:, :]

    @pl.when(pl.program_id(1) == pl.num_programs(1) - 1)
    def _():
        pool_ref[...] = halo[POOL_HALO - POOL_BUF:, :]

    mix = _pool_back(jnp.concatenate(pooled, axis=1), h, w_grp, scale, w_out)
    y_ref[...] = _dense_tail(x, mix, p_ref[...], *dense)


def _prompt_conformer_kernel(x_ref, p_ref, w_in, b_in, w_dw, b_dw, ln_g, ln_b, w_out, *rest):
    dense, (y_ref, conv_ref, halo) = rest[:8], rest[8:]
    tile = x_ref.shape[0]
    x = x_ref[...]
    g = _glu_front(x, w_in, b_in)

    @pl.when(pl.program_id(1) == 0)
    def _():
        halo[...] = jnp.zeros_like(halo)

    ext = jnp.concatenate([halo[...], g], axis=0)
    y = None
    for r in range(SUBLANES):
        ext_r = ext if r == 0 else pltpu.roll(ext, r, axis=0)
        for a in range((CONV_K - 1 - r) // SUBLANES + 1):
            k = CONV_K - 1 - (SUBLANES * a + r)
            start = CONV_HALO - SUBLANES * a
            term = ext_r[start:start + tile, :] * w_dw[k:k + 1, :]
            y = term if y is None else y + term
    halo[...] = g[tile - CONV_HALO:, :]

    @pl.when(pl.program_id(1) == pl.num_programs(1) - 1)
    def _():
        conv_ref[...] = halo[CONV_HALO - (CONV_K - 1):, :]

    mix = _conformer_back(y, b_dw, ln_g, ln_b, w_out)
    y_ref[...] = _dense_tail(x, mix, p_ref[...], *dense)


def _prompt_sconv_kernel(x_ref, p_ref, w_in, w_conv, w_out, *rest):
    dense, (y_ref, sconv_ref, halo) = rest[:8], rest[8:]
    tile = x_ref.shape[0]
    x = x_ref[...]
    z = _mm(x, w_in[...])
    bg = z[:, :D_MODEL]
    q = z[:, D_MODEL:2 * D_MODEL] * z[:, 2 * D_MODEL:]

    @pl.when(pl.program_id(1) == 0)
    def _():
        halo[...] = jnp.zeros_like(halo)

    ext = jnp.concatenate([halo[...], q], axis=0)
    y = q * w_conv[SCONV_K - 1:SCONV_K, :]
    for s in range(1, SCONV_K):
        k = SCONV_K - 1 - s
        y = y + pltpu.roll(ext, s, axis=0)[SCONV_HALO:, :] * w_conv[k:k + 1, :]
    halo[...] = q[tile - SCONV_HALO:, :]

    @pl.when(pl.program_id(1) == pl.num_programs(1) - 1)
    def _():
        sconv_ref[...] = halo[SCONV_HALO - (SCONV_K - 1):, :]

    mix = _mm(bg * y, w_out[...])
    y_ref[...] = _dense_tail(x, mix, p_ref[...], *dense)


def _hist(st_ref, k):
    return st_ref[:, k * D_MODEL:(k + 1) * D_MODEL]


def _sample_gmlp_kernel(x_ref, p_ref, w_in, b_in, ln_g, ln_b, w_s0, b_s0, w_out, *rest):
    dense, (y_ref, chunkv_ref) = rest[:8], rest[8:]
    x = x_ref[...]
    u, vn = _gmlp_front(x, w_in, b_in, ln_g, ln_b)
    s = jnp.concatenate(
        [vn[:, h * A_GROUP:(h + 1) * A_GROUP] * w_s0[h] + b_s0[h] for h in range(A_HEADS)], axis=1)
    chunkv_ref[...] = vn
    y_ref[...] = _dense_tail(x, _mm(u * s, w_out[...]), p_ref[...], *dense)


def _sample_pool_kernel(x_ref, p_ref, st_ref, w_in, w_grp, scale, w_out, *rest):
    dense, (y_ref, h_ref) = rest[:8], rest[8:]
    x = x_ref[...]
    h = _mm(x, w_in[...])
    pooled = []
    for g, w in enumerate(POOL_WINDOWS):
        cols = slice(g * B_GROUP, (g + 1) * B_GROUP)
        acc = h[:, cols]
        for k in range(POOL_BUF - (w - 1), POOL_BUF):
            acc = acc + _hist(st_ref, k)[:, cols]
        pooled.append(acc / float(min(w, PAST_LEN + 1)))
    h_ref[...] = h
    mix = _pool_back(jnp.concatenate(pooled, axis=1), h, w_grp, scale, w_out)
    y_ref[...] = _dense_tail(x, mix, p_ref[...], *dense)


def _sample_conformer_kernel(x_ref, p_ref, st_ref, w_in, b_in, w_dw, b_dw, ln_g, ln_b, w_out, *rest):
    dense, (y_ref, g_ref) = rest[:8], rest[8:]
    x = x_ref[...]
    g = _glu_front(x, w_in, b_in)
    y = g * w_dw[CONV_K - 1:CONV_K, :]
    for k in range(CONV_K - 1):
        y = y + _hist(st_ref, k) * w_dw[k:k + 1, :]
    g_ref[...] = g
    mix = _conformer_back(y, b_dw, ln_g, ln_b, w_out)
    y_ref[...] = _dense_tail(x, mix, p_ref[...], *dense)


def _sample_sconv_kernel(x_ref, p_ref, st_ref, w_in, w_conv, w_out, *rest):
    dense, (y_ref, q_ref) = rest[:8], rest[8:]
    x = x_ref[...]
    z = _mm(x, w_in[...])
    bg = z[:, :D_MODEL]
    q = z[:, D_MODEL:2 * D_MODEL] * z[:, 2 * D_MODEL:]
    y = q * w_conv[SCONV_K - 1:SCONV_K, :]
    for k in range(SCONV_K - 1):
        y = y + _hist(st_ref, k) * w_conv[k:k + 1, :]
    q_ref[...] = q
    y_ref[...] = _dense_tail(x, _mm(bg * y, w_out[...]), p_ref[...], *dense)


def _resident(arr, lead=None):
    if lead is None:
        shape, idx = arr.shape, (0,) * arr.ndim
    else:
        shape, idx = (None,) + arr.shape[1:], (lead,) + (0,) * (arr.ndim - 1)
    return pl.BlockSpec(shape, lambda *_: idx, pipeline_mode=pl.Buffered(1))


def _smem():
    return pl.BlockSpec(memory_space=pltpu.SMEM)


def _dense_operands(layer, prm):
    arrs = [prm['ln1_g'], prm['ln1_b'], prm['mlp_w_up'], prm['mlp_w_down'],
            prm['ln2_g'], prm['ln2_b'], prm['ple_w_proj'], prm['ple_w_gate']]
    return arrs, [_resident(a, lead=layer) for a in arrs]


def _prompt_layer(layer, x, p, prm):
    batch, seq, _ = x.shape
    tile = PROMPT_TILE
    assert seq % tile == 0 and tile % CHUNK == 0 and tile >= CONV_HALO
    kind = layer % 4
    if kind == 0:
        body, aux_rows, halo_rows = _prompt_gmlp_kernel, CHUNK, None
        mixer = [prm['a_w_in'], prm['a_b_in'], prm['a_ln_g'], prm['a_ln_b'],
                 prm['a_w_s'], prm['a_b_s3'], prm['a_w_out']]
    elif kind == 1:
        body, aux_rows, halo_rows = _prompt_pool_kernel, POOL_BUF, POOL_HALO
        mixer = [prm['b_w_in'], prm['b_w_grp'], prm['b_scale'], prm['b_w_out']]
    elif kind == 2:
        body, aux_rows, halo_rows = _prompt_conformer_kernel, CONV_K - 1, CONV_HALO
        mixer = [prm['c_w_in'], prm['c_b_in'], prm['c_w_dw'], prm['c_b_dw'],
                 prm['c_ln_g'], prm['c_ln_b'], prm['c_w_out']]
    else:
        body, aux_rows, halo_rows = _prompt_sconv_kernel, SCONV_K - 1, SCONV_HALO
        mixer = [prm['d_w_in'], prm['d_w_conv'], prm['d_w_out']]
    dense, dense_specs = _dense_operands(layer, prm)
    in_specs = [
        pl.BlockSpec((None, tile, D_MODEL), lambda b, t: (b, t, 0)),
        pl.BlockSpec((None, None, tile, D_PLE), lambda b, t: (layer, b, t, 0)),
    ] + [_resident(a) for a in mixer] + dense_specs
    out_shape = (jax.ShapeDtypeStruct(x.shape, F32),
                 jax.ShapeDtypeStruct((batch, aux_rows, D_MODEL), F32))
    out_specs = (pl.BlockSpec((None, tile, D_MODEL), lambda b, t: (b, t, 0)),
                 pl.BlockSpec((None, aux_rows, D_MODEL), lambda b, t: (b, 0, 0)))
    scratch = [] if halo_rows is None else [pltpu.VMEM((halo_rows, D_MODEL), F32)]
    return pl.pallas_call(
        body,
        out_shape=out_shape,
        grid=(batch, seq // tile),
        in_specs=in_specs,
        out_specs=out_specs,
        scratch_shapes=scratch,
        compiler_params=pltpu.CompilerParams(
            dimension_semantics=("arbitrary", "arbitrary"),
            vmem_limit_bytes=VMEM_LIMIT_BYTES),
        name=f"prompt_layer{layer}",
    )(x, p, *mixer, *dense)


def _sample_layer(layer, x, p, state, prm):
    rows = x.shape[0]
    kind = layer % 4
    if kind == 0:
        body = _sample_gmlp_kernel
        mixer = [prm['a_w_in'], prm['a_b_in'], prm['a_ln_g'], prm['a_ln_b'],
                 prm['a_w_s00'], prm['a_b_s0'], prm['a_w_out']]
        mixer_specs = [_resident(a) for a in mixer[:4]] + [_smem(), _smem(), _resident(mixer[6])]
    else:
        if kind == 1:
            body = _sample_pool_kernel
            mixer = [state, prm['b_w_in'], prm['b_w_grp'], prm['b_scale'], prm['b_w_out']]
        elif kind == 2:
            body = _sample_conformer_kernel
            mixer = [state, prm['c_w_in'], prm['c_b_in'], prm['c_w_dw'], prm['c_b_dw'],
                     prm['c_ln_g'], prm['c_ln_b'], prm['c_w_out']]
        else:
            body = _sample_sconv_kernel
            mixer = [state, prm['d_w_in'], prm['d_w_conv'], prm['d_w_out']]
        mixer_specs = [_resident(a) for a in mixer]
    dense, dense_specs = _dense_operands(layer, prm)
    in_specs = [_resident(x), _resident(p, lead=layer)] + mixer_specs + dense_specs
    out_shape = (jax.ShapeDtypeStruct((rows, D_MODEL), F32),) * 2
    out_specs = (pl.BlockSpec((rows, D_MODEL), lambda i: (0, 0)),) * 2
    return pl.pallas_call(
        body,
        out_shape=out_shape,
        grid=(1,),
        in_specs=in_specs,
        out_specs=out_specs,
        compiler_params=pltpu.CompilerParams(
            dimension_semantics=("arbitrary",),
            vmem_limit_bytes=VMEM_LIMIT_BYTES),
        name=f"sample_layer{layer}",
    )(x, p, *mixer, *dense)


def _row(v):
    return v.reshape(1, -1)


def kernel(x_prompt, x_sample, state_pool, state_conv, state_shortconv, p_prompt, p_sample, a_w_in, a_b_in, a_ln_g, a_ln_b, a_w_s, a_b_s, a_w_out, b_w_in, b_w_grp, b_scale, b_w_out, c_w_in, c_b_in, c_w_dw, c_b_dw, c_ln_g, c_ln_b, c_w_out, d_w_in, d_w_conv, d_w_out, ln1_g, ln1_b, ln2_g, ln2_b, mlp_w_up, mlp_w_down, ple_w_proj, ple_w_gate):
    rows = x_sample.shape[0]
    depth = mlp_w_up.shape[0]
    prm = {
        'a_w_in': a_w_in.astype(BF16), 'a_w_out': a_w_out.astype(BF16),
        'b_w_in': b_w_in.astype(BF16), 'b_w_grp': b_w_grp.astype(BF16), 'b_w_out': b_w_out.astype(BF16),
        'c_w_in': c_w_in.astype(BF16), 'c_w_out': c_w_out.astype(BF16),
        'd_w_in': d_w_in.astype(BF16), 'd_w_out': d_w_out.astype(BF16),
        'mlp_w_up': mlp_w_up.astype(BF16), 'mlp_w_down': mlp_w_down.astype(BF16),
        'ple_w_proj': ple_w_proj.astype(BF16), 'ple_w_gate': ple_w_gate.astype(BF16),
        'a_b_in': _row(a_b_in), 'a_ln_g': _row(a_ln_g), 'a_ln_b': _row(a_ln_b),
        'a_w_s': a_w_s, 'a_b_s3': a_b_s[:, :, None],
        'a_w_s00': a_w_s[:, 0, 0], 'a_b_s0': a_b_s[:, 0],
        'b_scale': _row(b_scale),
        'c_b_in': _row(c_b_in), 'c_w_dw': c_w_dw, 'c_b_dw': _row(c_b_dw),
        'c_ln_g': _row(c_ln_g), 'c_ln_b': _row(c_ln_b),
        'd_w_conv': d_w_conv,
        'ln1_g': ln1_g[:, None, :], 'ln1_b': ln1_b[:, None, :],
        'ln2_g': ln2_g[:, None, :], 'ln2_b': ln2_b[:, None, :],
    }

    aux_prompt = []
    x = x_prompt
    for layer in range(depth):
        x, aux = _prompt_layer(layer, x, p_prompt, prm)
        aux_prompt.append(aux)
    y_prompt = x

    states = [None, state_pool.reshape(rows, -1), state_conv.reshape(rows, -1),
              state_shortconv.reshape(rows, -1)]
    p_s = p_sample.reshape(depth, rows, D_PLE)
    aux_sample = []
    x = x_sample.reshape(rows, D_MODEL)
    for layer in range(depth):
        x, aux = _sample_layer(layer, x, p_s, states[layer % 4], prm)
        aux_sample.append(aux[:, None, :])
    y_sample = x[:, None, :]

    def shifted(state, new_row):
        return jnp.concatenate([state[:, 1:, :], new_row], axis=1)

    return (y_prompt, y_sample, aux_prompt[0], aux_sample[0],
            aux_prompt[1], shifted(state_pool, aux_sample[1]),
            aux_prompt[2], shifted(state_conv, aux_sample[2]),
            aux_prompt[3], shifted(state_shortconv, aux_sample[3]))
```

```python
import functools

import jax
import jax.numpy as jnp
from jax import lax
from jax.experimental import pallas as pl
from jax.experimental.pallas import tpu as pltpu

D_MODEL = 1024
D_PLE = 256
D_FF = 4 * D_MODEL
DEPTH = 4
PAST_LEN = 16384
LN_EPS = 1e-5
DEEPNORM_ALPHA = (2 * DEPTH) ** 0.25
CHUNK = 128
A_GROUP = 128
A_HEADS = D_MODEL // A_GROUP
POOL_WINDOWS = (2, 4, 8, 16)
B_GROUP = D_MODEL // len(POOL_WINDOWS)
POOL_BUF = max(POOL_WINDOWS) - 1
CONV_K = 31
SCONV_K = 3

SUBLANES = 8
MXU_DIM = 256
N_MXU = 2
V7X_VMEM_BYTES = 64 * 1024 * 1024
VMEM_LIMIT_BYTES = V7X_VMEM_BYTES - 8 * 1024 * 1024

PROMPT_TILE = 512
PROMPT_SUB = 256
ROW_PIECE = 32
COL_PIECE = 512
MIX_PIECE = 256
POOL_HALO = 16
CONV_HALO = 32
SCONV_HALO = 8
N_DENSE = 8

F32 = jnp.float32
BF16 = jnp.bfloat16


def _dot(a, b):
    return jnp.dot(a, b, preferred_element_type=F32)


def _mm(a, w):
    return _dot(a.astype(BF16), w)


def _layer_norm(x, g, b):
    mu = jnp.mean(x, axis=-1, keepdims=True)
    xc = x - mu
    var = jnp.mean(xc * xc, axis=-1, keepdims=True)
    return xc * lax.rsqrt(var + LN_EPS) * g + b


def _mxu_cycles(rows, k, n):
    tiles = -(-k // MXU_DIM) * -(-n // MXU_DIM)
    return rows // 2 * tiles // N_MXU


def _vpu_cycles(rows, cols, ops_per_vreg):
    return rows * cols * ops_per_vreg // (SUBLANES * 128 * 4)


def _run(gen):
    try:
        while True:
            next(gen)
    except StopIteration as stop:
        return stop.value


def _interleave(gens):
    results = [None] * len(gens)
    pending = {i: next(gen) for i, gen in enumerate(gens)}
    done = [0] * len(gens)
    ready = [0] * len(gens)
    free = [0, 0]

    def starts(i):
        m, v = pending[i]
        t_m = max(ready[i], free[0]) if m > 0 else ready[i]
        t_v = max(t_m, free[1]) if v > 0 else t_m
        return t_m, t_v

    while pending:
        allowed = [i for i in pending if i == 0 or (i - 1) not in pending or done[i] < done[i - 1]]
        i = min(allowed, key=lambda k: (starts(k)[0 if pending[k][0] > 0 else 1], k))
        (m, v), (t_m, t_v) = pending[i], starts(i)
        if m > 0:
            free[0] = t_m + m
        if v > 0:
            free[1] = t_v + v
        ready[i] = max(t_m + m, t_v + v)
        done[i] += 1
        try:
            pending[i] = next(gens[i])
        except StopIteration as stop:
            results[i] = stop.value
            del pending[i]
    return results


def _matmul_stage(lhs, w_ref, col_piece=COL_PIECE):
    rows, k = lhs.shape
    tiles = []
    for n0 in range(0, w_ref.shape[1], col_piece):
        yield _mxu_cycles(rows, k, col_piece), 0
        tiles.append(_dot(lhs, w_ref[:, n0:n0 + col_piece]))
    return jnp.concatenate(tiles, axis=1)


def _norm_stage(pre, g_ref, b_ref, rows, post=None, ops=14):
    outs, outs_b = [], []
    for r0 in range(0, rows, ROW_PIECE):
        yield 0, _vpu_cycles(ROW_PIECE, D_MODEL, ops)
        rs = slice(r0, r0 + ROW_PIECE)
        y = _layer_norm(pre(rs), g_ref[...], b_ref[...])
        outs.append(y)
        outs_b.append((y if post is None else post(y)).astype(BF16))
    return jnp.concatenate(outs, axis=0), jnp.concatenate(outs_b, axis=0)


def _dense_tail(x_ref, p_ref, y_ref, rows, mix, ln1_g, ln1_b, w_up, w_down, ln2_g, ln2_b, w_proj, w_gate):
    n = rows.stop - rows.start

    def pre1(rs):
        return DEEPNORM_ALPHA * x_ref[rows.start + rs.start:rows.start + rs.stop, :] + mix[rs]

    x1, x1b = yield from _norm_stage(pre1, ln1_g, ln1_b, n)
    hid = []
    for n0 in range(0, D_FF, COL_PIECE):
        yield _mxu_cycles(n, D_MODEL, COL_PIECE), _vpu_cycles(n, COL_PIECE, 5)
        h = _dot(x1b, w_up[:, n0:n0 + COL_PIECE])
        hid.append(jnp.square(jnp.maximum(h, 0.0)).astype(BF16))
    down = yield from _matmul_stage(jnp.concatenate(hid, axis=1), w_down)

    def pre2(rs):
        return DEEPNORM_ALPHA * x1[rs] + down[rs]

    x2, x2b = yield from _norm_stage(pre2, ln2_g, ln2_b, n)
    for n0 in range(0, D_MODEL, COL_PIECE):
        cs = slice(n0, n0 + COL_PIECE)
        yield (_mxu_cycles(n, D_MODEL, COL_PIECE) + _mxu_cycles(n, D_PLE, COL_PIECE),
               _vpu_cycles(n, COL_PIECE, 7))
        gate = jax.nn.sigmoid(_dot(x2b, w_gate[:, cs]))
        y_ref[rows, cs] = x2[:, cs] + _mm(p_ref[rows, :], w_proj[:, cs]) * gate


def _gmlp_block(x_ref, rows, pos0, hist, keep, causal, w_in, b_in, ln_g, ln_b, b_s, w_out):
    n = rows.stop - rows.start
    z, xb = [], None
    for n0 in range(0, 2 * D_MODEL, COL_PIECE):
        yield _mxu_cycles(n, D_MODEL, COL_PIECE), _vpu_cycles(n, COL_PIECE, 11)
        if xb is None:
            xb = x_ref[rows, :].astype(BF16)
        cs = slice(n0, n0 + COL_PIECE)
        z.append(jax.nn.gelu(_dot(xb, w_in[:, cs]) + b_in[:, cs], approximate=True))
    z = jnp.concatenate(z, axis=1)
    u, v = z[:, :D_MODEL], z[:, D_MODEL:]
    vn, vnb = yield from _norm_stage(lambda rs: v[rs], ln_g, ln_b, n, ops=11)
    gated = []
    for c in range(n // CHUNK):
        rs = slice(c * CHUNK, (c + 1) * CHUNK)
        heads = []
        for h in range(A_HEADS):
            if h % 2 == 0:
                yield 2 * _mxu_cycles(CHUNK, A_GROUP, A_GROUP), 2 * _vpu_cycles(CHUNK, A_GROUP, 4)
            cs = slice(h * A_GROUP, (h + 1) * A_GROUP)
            s = _dot(causal[h], vnb[rs, cs]) + b_s[h]
            heads.append((u[rs, cs] * s).astype(BF16))
        gated.append(jnp.concatenate(heads, axis=1))
    mix = yield from _matmul_stage(jnp.concatenate(gated, axis=0), w_out)
    return mix, vn[n - CHUNK:, :]


def _pool_block(x_ref, rows, pos0, hist, keep, w_in, w_grp, scale, w_out):
    n = rows.stop - rows.start
    h, xb = [], None
    for n0 in range(0, D_MODEL, COL_PIECE):
        yield _mxu_cycles(n, D_MODEL, COL_PIECE), 0
        if xb is None:
            xb = x_ref[rows, :].astype(BF16)
        h.append(_dot(xb, w_in[:, n0:n0 + COL_PIECE]))
    h = jnp.concatenate(h, axis=1)
    pos1 = (pos0 + 1 + lax.broadcasted_iota(jnp.int32, (n, 1), 0)).astype(F32)
    mixed = []
    for g, w in enumerate(POOL_WINDOWS):
        assert w & (w - 1) == 0 and w <= POOL_HALO
        yield _mxu_cycles(n, B_GROUP, B_GROUP), _vpu_cycles(n, B_GROUP, 4 * w.bit_length() + 4)
        cs = slice(g * B_GROUP, (g + 1) * B_GROUP)
        hg = h[:, cs]
        keep(cs, hg[n - POOL_HALO:, :])
        acc = jnp.concatenate([hist(cs), hg], axis=0)
        step = 1
        while step < w:
            acc = acc + pltpu.roll(acc, step, axis=0)
            step *= 2
        d = acc[POOL_HALO:, :] / jnp.minimum(float(w), pos1) - hg
        mixed.append((_mm(d, w_grp[g]) * scale[:, cs]).astype(BF16))
    mix = yield from _matmul_stage(jnp.concatenate(mixed, axis=1), w_out)
    return mix, None


def _conformer_block(x_ref, rows, pos0, hist, keep, w_in, b_in, w_dw, b_dw, ln_g, ln_b, w_out):
    n = rows.stop - rows.start
    y, xb = [], None
    for n0 in range(0, D_MODEL, MIX_PIECE):
        yield 2 * _mxu_cycles(n, D_MODEL, MIX_PIECE), _vpu_cycles(n, MIX_PIECE, 7)
        if xb is None:
            xb = x_ref[rows, :].astype(BF16)
        cs = slice(n0, n0 + MIX_PIECE)
        cs2 = slice(D_MODEL + n0, D_MODEL + n0 + MIX_PIECE)
        a = _dot(xb, w_in[:, cs]) + b_in[:, cs]
        g = a * jax.nn.sigmoid(_dot(xb, w_in[:, cs2]) + b_in[:, cs2])
        keep(cs, g[n - CONV_HALO:, :])
        ext = jnp.concatenate([hist(cs), g], axis=0)
        acc = None
        for r in range(SUBLANES):
            taps = (CONV_K - 1 - r) // SUBLANES + 1
            yield 0, _vpu_cycles(n, MIX_PIECE, 2 * taps + 3)
            ext_r = ext if r == 0 else pltpu.roll(ext, r, axis=0)
            for a_ in range(taps):
                k = CONV_K - 1 - (SUBLANES * a_ + r)
                start = CONV_HALO - SUBLANES * a_
                term = ext_r[start:start + n, :] * w_dw[k:k + 1, cs]
                acc = term if acc is None else acc + term
        y.append(acc)
    y = jnp.concatenate(y, axis=1)
    _, act = yield from _norm_stage(lambda rs: y[rs] + b_dw[...], ln_g, ln_b, n,
                                    post=lambda yn: yn * jax.nn.sigmoid(yn), ops=16)
    mix = yield from _matmul_stage(act, w_out)
    return mix, None


def _sconv_block(x_ref, rows, pos0, hist, keep, w_in, w_conv, w_out):
    n = rows.stop - rows.start
    gated, xb = [], None
    for n0 in range(0, D_MODEL, MIX_PIECE):
        yield 3 * _mxu_cycles(n, D_MODEL, MIX_PIECE), _vpu_cycles(n, MIX_PIECE, 14)
        if xb is None:
            xb = x_ref[rows, :].astype(BF16)
        cs = slice(n0, n0 + MIX_PIECE)
        bg, cg, hh = (_dot(xb, w_in[:, i * D_MODEL + n0:i * D_MODEL + n0 + MIX_PIECE]) for i in range(3))
        q = cg * hh
        keep(cs, q[n - SCONV_HALO:, :])
        ext = jnp.concatenate([hist(cs), q], axis=0)
        y = q * w_conv[SCONV_K - 1:SCONV_K, cs]
        for s in range(1, SCONV_K):
            k = SCONV_K - 1 - s
            y = y + pltpu.roll(ext, s, axis=0)[SCONV_HALO:, :] * w_conv[k:k + 1, cs]
        gated.append((bg * y).astype(BF16))
    mix = yield from _matmul_stage(jnp.concatenate(gated, axis=1), w_out)
    return mix, None


def _prompt_kernel(block_fn, n_mixer, sub, x_ref, p_ref, *rest):
    mixer, dense = rest[:n_mixer], rest[n_mixer:n_mixer + N_DENSE]
    y_ref, aux_ref, *scratch = rest[n_mixer + N_DENSE:]
    tile = x_ref.shape[0]
    n_blocks = tile // sub
    halo = scratch[0] if scratch else None
    if halo is not None:
        @pl.when(pl.program_id(1) == 0)
        def _():
            halo[...] = jnp.zeros_like(halo)

    if block_fn is _gmlp_block:
        w_s, mixer = mixer[0], mixer[1:]
        row = lax.broadcasted_iota(jnp.int32, (CHUNK, CHUNK), 0)
        col = lax.broadcasted_iota(jnp.int32, (CHUNK, CHUNK), 1)
        mixer = ([jnp.where(row >= col, w_s[h], 0.0).astype(BF16) for h in range(A_HEADS)],) + tuple(mixer)

    handoff = [dict() for _ in range(n_blocks + 1)]

    def block(k):
        rows = slice(k * sub, (k + 1) * sub)
        pos0 = pl.program_id(1) * tile + k * sub

        def hist(cs):
            return halo[:, cs] if k == 0 else handoff[k][cs.start]

        def keep(cs, value):
            if k == n_blocks - 1:
                halo[:, cs] = value
            else:
                handoff[k + 1][cs.start] = value

        mix, tail = yield from block_fn(x_ref, rows, pos0, hist, keep, *mixer)
        yield from _dense_tail(x_ref, p_ref, y_ref, rows, mix, *dense)
        return tail

    tails = _interleave([block(k) for k in range(n_blocks)])

    @pl.when(pl.program_id(1) == pl.num_programs(1) - 1)
    def _():
        if halo is not None:
            aux_ref[...] = halo[halo.shape[0] - aux_ref.shape[0]:, :]
        else:
            aux_ref[...] = tails[-1]


def _hist_row(st_ref, k):
    return st_ref[:, k * D_MODEL:(k + 1) * D_MODEL]


def _sample_tail(x_ref, p_ref, y_ref, mix, dense):
    _run(_dense_tail(x_ref, p_ref, y_ref, slice(0, x_ref.shape[0]), mix, *dense))


def _sample_gmlp_kernel(x_ref, p_ref, w_in, b_in, ln_g, ln_b, w_s0, b_s0, w_out, *rest):
    dense, (y_ref, chunkv_ref) = rest[:N_DENSE], rest[N_DENSE:]
    z = jax.nn.gelu(_mm(x_ref[...], w_in[...]) + b_in[...], approximate=True)
    u = z[:, :D_MODEL]
    vn = _layer_norm(z[:, D_MODEL:], ln_g[...], ln_b[...])
    s = jnp.concatenate(
        [vn[:, h * A_GROUP:(h + 1) * A_GROUP] * w_s0[h] + b_s0[h] for h in range(A_HEADS)], axis=1)
    chunkv_ref[...] = vn
    _sample_tail(x_ref, p_ref, y_ref, _mm(u * s, w_out[...]), dense)


def _sample_pool_kernel(x_ref, p_ref, st_ref, w_in, w_grp, scale, w_out, *rest):
    dense, (y_ref, h_ref) = rest[:N_DENSE], rest[N_DENSE:]
    h = _mm(x_ref[...], w_in[...])
    mixed = []
    for g, w in enumerate(POOL_WINDOWS):
        cols = slice(g * B_GROUP, (g + 1) * B_GROUP)
        acc = h[:, cols]
        for k in range(POOL_BUF - (w - 1), POOL_BUF):
            acc = acc + _hist_row(st_ref, k)[:, cols]
        d = acc / float(min(w, PAST_LEN + 1)) - h[:, cols]
        mixed.append(_mm(d, w_grp[g]))
    h_ref[...] = h
    mix = _mm(jnp.concatenate(mixed, axis=1) * scale[...], w_out[...])
    _sample_tail(x_ref, p_ref, y_ref, mix, dense)


def _sample_conformer_kernel(x_ref, p_ref, st_ref, w_in, b_in, w_dw, b_dw, ln_g, ln_b, w_out, *rest):
    dense, (y_ref, g_ref) = rest[:N_DENSE], rest[N_DENSE:]
    z = _mm(x_ref[...], w_in[...]) + b_in[...]
    g = z[:, :D_MODEL] * jax.nn.sigmoid(z[:, D_MODEL:])
    y = g * w_dw[CONV_K - 1:CONV_K, :]
    for k in range(CONV_K - 1):
        y = y + _hist_row(st_ref, k) * w_dw[k:k + 1, :]
    g_ref[...] = g
    yn = _layer_norm(y + b_dw[...], ln_g[...], ln_b[...])
    mix = _mm(yn * jax.nn.sigmoid(yn), w_out[...])
    _sample_tail(x_ref, p_ref, y_ref, mix, dense)


def _sample_sconv_kernel(x_ref, p_ref, st_ref, w_in, w_conv, w_out, *rest):
    dense, (y_ref, q_ref) = rest[:N_DENSE], rest[N_DENSE:]
    z = _mm(x_ref[...], w_in[...])
    bg = z[:, :D_MODEL]
    q = z[:, D_MODEL:2 * D_MODEL] * z[:, 2 * D_MODEL:]
    y = q * w_conv[SCONV_K - 1:SCONV_K, :]
    for k in range(SCONV_K - 1):
        y = y + _hist_row(st_ref, k) * w_conv[k:k + 1, :]
    q_ref[...] = q
    _sample_tail(x_ref, p_ref, y_ref, _mm(bg * y, w_out[...]), dense)


def _resident(arr, lead=None):
    if lead is None:
        shape, idx = arr.shape, (0,) * arr.ndim
    else:
        shape, idx = (None,) + arr.shape[1:], (lead,) + (0,) * (arr.ndim - 1)
    return pl.BlockSpec(shape, lambda *_: idx, pipeline_mode=pl.Buffered(1))


def _smem():
    return pl.BlockSpec(memory_space=pltpu.SMEM)


def _dense_operands(layer, prm):
    arrs = [prm['ln1_g'], prm['ln1_b'], prm['mlp_w_up'], prm['mlp_w_down'],
            prm['ln2_g'], prm['ln2_b'], prm['ple_w_proj'], prm['ple_w_gate']]
    assert len(arrs) == N_DENSE
    return arrs, [_resident(a, lead=layer) for a in arrs]


def _prompt_layer(layer, x, p, prm):
    batch, seq, _ = x.shape
    tile, sub = PROMPT_TILE, PROMPT_SUB
    assert seq % tile == 0 and tile % sub == 0 and sub % CHUNK == 0 and sub >= CONV_HALO
    kind = layer % 4
    if kind == 0:
        block_fn, aux_rows, halo_rows = _gmlp_block, CHUNK, None
        mixer = [prm['a_w_s'], prm['a_w_in'], prm['a_b_in'], prm['a_ln_g'], prm['a_ln_b'],
                 prm['a_b_s3'], prm['a_w_out']]
    elif kind == 1:
        block_fn, aux_rows, halo_rows = _pool_block, POOL_BUF, POOL_HALO
        mixer = [prm['b_w_in'], prm['b_w_grp'], prm['b_scale'], prm['b_w_out']]
    elif kind == 2:
        block_fn, aux_rows, halo_rows = _conformer_block, CONV_K - 1, CONV_HALO
        mixer = [prm['c_w_in'], prm['c_b_in'], prm['c_w_dw'], prm['c_b_dw'],
                 prm['c_ln_g'], prm['c_ln_b'], prm['c_w_out']]
    else:
        block_fn, aux_rows, halo_rows = _sconv_block, SCONV_K - 1, SCONV_HALO
        mixer = [prm['d_w_in'], prm['d_w_conv'], prm['d_w_out']]
    dense, dense_specs = _dense_operands(layer, prm)
    in_specs = [
        pl.BlockSpec((None, tile, D_MODEL), lambda b, t: (b, t, 0)),
        pl.BlockSpec((None, None, tile, D_PLE), lambda b, t: (layer, b, t, 0)),
    ] + [_resident(a) for a in mixer] + dense_specs
    out_shape = (jax.ShapeDtypeStruct(x.shape, F32),
                 jax.ShapeDtypeStruct((batch, aux_rows, D_MODEL), F32))
    out_specs = (pl.BlockSpec((None, tile, D_MODEL), lambda b, t: (b, t, 0)),
                 pl.BlockSpec((None, aux_rows, D_MODEL), lambda b, t: (b, 0, 0)))
    scratch = [] if halo_rows is None else [pltpu.VMEM((halo_rows, D_MODEL), F32)]
    return pl.pallas_call(
        functools.partial(_prompt_kernel, block_fn, len(mixer), sub),
        out_shape=out_shape,
        grid=(batch, seq // tile),
        in_specs=in_specs,
        out_specs=out_specs,
        scratch_shapes=scratch,
        compiler_params=pltpu.CompilerParams(
            dimension_semantics=("arbitrary", "arbitrary"),
            vmem_limit_bytes=VMEM_LIMIT_BYTES),
        name=f"prompt_layer{layer}",
    )(x, p, *mixer, *dense)


def _sample_layer(layer, x, p, state, prm):
    rows = x.shape[0]
    kind = layer % 4
    if kind == 0:
        body = _sample_gmlp_kernel
        mixer = [prm['a_w_in'], prm['a_b_in'], prm['a_ln_g'], prm['a_ln_b'],
                 prm['a_w_s00'], prm['a_b_s0'], prm['a_w_out']]
        mixer_specs = [_resident(a) for a in mixer[:4]] + [_smem(), _smem(), _resident(mixer[6])]
    else:
        if kind == 1:
            body = _sample_pool_kernel
            mixer = [state, prm['b_w_in'], prm['b_w_grp'], prm['b_scale'], prm['b_w_out']]
        elif kind == 2:
            body = _sample_conformer_kernel
            mixer = [state, prm['c_w_in'], prm['c_b_in'], prm['c_w_dw'], prm['c_b_dw'],
                     prm['c_ln_g'], prm['c_ln_b'], prm['c_w_out']]
        else:
            body = _sample_sconv_kernel
            mixer = [state, prm['d_w_in'], prm['d_w_conv'], prm['d_w_out']]
        mixer_specs = [_resident(a) for a in mixer]
    dense, dense_specs = _dense_operands(layer, prm)
    in_specs = [_resident(x), _resident(p, lead=layer)] + mixer_specs + dense_specs
    out_shape = (jax.ShapeDtypeStruct((rows, D_MODEL), F32),) * 2
    out_specs = (pl.BlockSpec((rows, D_MODEL), lambda i: (0, 0)),) * 2
    return pl.pallas_call(
        body,
        out_shape=out_shape,
        grid=(1,),
        in_specs=in_specs,
        out_specs=out_specs,
        compiler_params=pltpu.CompilerParams(
            dimension_semantics=("arbitrary",),
            vmem_limit_bytes=VMEM_LIMIT_BYTES),
        name=f"sample_layer{layer}",
    )(x, p, *mixer, *dense)


def _row(v):
    return v.reshape(1, -1)


def kernel(x_prompt, x_sample, state_pool, state_conv, state_shortconv, p_prompt, p_sample, a_w_in, a_b_in, a_ln_g, a_ln_b, a_w_s, a_b_s, a_w_out, b_w_in, b_w_grp, b_scale, b_w_out, c_w_in, c_b_in, c_w_dw, c_b_dw, c_ln_g, c_ln_b, c_w_out, d_w_in, d_w_conv, d_w_out, ln1_g, ln1_b, ln2_g, ln2_b, mlp_w_up, mlp_w_down, ple_w_proj, ple_w_gate):
    rows = x_sample.shape[0]
    depth = mlp_w_up.shape[0]
    prm = {
        'a_w_in': a_w_in.astype(BF16), 'a_w_out': a_w_out.astype(BF16),
        'b_w_in': b_w_in.astype(BF16), 'b_w_grp': b_w_grp.astype(BF16), 'b_w_out': b_w_out.astype(BF16),
        'c_w_in': c_w_in.astype(BF16), 'c_w_out': c_w_out.astype(BF16),
        'd_w_in': d_w_in.astype(BF16), 'd_w_out': d_w_out.astype(BF16),
        'mlp_w_up': mlp_w_up.astype(BF16), 'mlp_w_down': mlp_w_down.astype(BF16),
        'ple_w_proj': ple_w_proj.astype(BF16), 'ple_w_gate': ple_w_gate.astype(BF16),
        'a_b_in': _row(a_b_in), 'a_ln_g': _row(a_ln_g), 'a_ln_b': _row(a_ln_b),
        'a_w_s': a_w_s, 'a_b_s3': a_b_s[:, :, None],
        'a_w_s00': a_w_s[:, 0, 0], 'a_b_s0': a_b_s[:, 0],
        'b_scale': _row(b_scale),
        'c_b_in': _row(c_b_in), 'c_w_dw': c_w_dw, 'c_b_dw': _row(c_b_dw),
        'c_ln_g': _row(c_ln_g), 'c_ln_b': _row(c_ln_b),
        'd_w_conv': d_w_conv,
        'ln1_g': ln1_g[:, None, :], 'ln1_b': ln1_b[:, None, :],
        'ln2_g': ln2_g[:, None, :], 'ln2_b': ln2_b[:, None, :],
    }

    aux_prompt = []
    x = x_prompt
    for layer in range(depth):
        x, aux = _prompt_layer(layer, x, p_prompt, prm)
        aux_prompt.append(aux)
    y_prompt = x

    states = [None, state_pool.reshape(rows, -1), state_conv.reshape(rows, -1),
              state_shortconv.reshape(rows, -1)]
    p_s = p_sample.reshape(depth, rows, D_PLE)
    aux_sample = []
    x = x_sample.reshape(rows, D_MODEL)
    for layer in range(depth):
        x, aux = _sample_layer(layer, x, p_s, states[layer % 4], prm)
        aux_sample.append(aux[:, None, :])
    y_sample = x[:, None, :]

    def shifted(state, new_row):
        return jnp.concatenate([state[:, 1:, :], new_row], axis=1)

    return (y_prompt, y_sample, aux_prompt[0], aux_sample[0],
            aux_prompt[1], shifted(state_pool, aux_sample[1]),
            aux_prompt[2], shifted(state_conv, aux_sample[2]),
            aux_prompt[3], shifted(state_shortconv, aux_sample[3]))
```

```python
import functools

import jax
import jax.numpy as jnp
from jax import lax
from jax.experimental import pallas as pl
from jax.experimental.pallas import tpu as pltpu

D_MODEL = 1024
D_PLE = 256
D_FF = 4 * D_MODEL
DEPTH = 4
PAST_LEN = 16384
LN_EPS = 1e-5
DEEPNORM_ALPHA = (2 * DEPTH) ** 0.25
CHUNK = 128
A_GROUP = 128
A_HEADS = D_MODEL // A_GROUP
POOL_WINDOWS = (2, 4, 8, 16)
B_GROUP = D_MODEL // len(POOL_WINDOWS)
POOL_BUF = max(POOL_WINDOWS) - 1
CONV_K = 31
SCONV_K = 3

SUBLANES = 8
MXU_DIM = 256
N_MXU = 2
V7X_VMEM_BYTES = 64 * 1024 * 1024
VMEM_LIMIT_BYTES = V7X_VMEM_BYTES - 8 * 1024 * 1024

PROMPT_TILE = 512
PROMPT_SUB = 256
ROW_PIECE = 32
COL_PIECE = 512
MIX_PIECE = 256
POOL_HALO = 16
CONV_HALO = 32
SCONV_HALO = 8
N_DENSE = 8

F32 = jnp.float32
BF16 = jnp.bfloat16


def _dot(a, b):
    return jnp.dot(a, b, preferred_element_type=F32)


def _mm(a, w):
    return _dot(a.astype(BF16), w)


def _layer_norm(x, g, b):
    mu = jnp.mean(x, axis=-1, keepdims=True)
    xc = x - mu
    var = jnp.mean(xc * xc, axis=-1, keepdims=True)
    return xc * lax.rsqrt(var + LN_EPS) * g + b


def _mxu_cycles(rows, k, n):
    tiles = -(-k // MXU_DIM) * -(-n // MXU_DIM)
    return rows // 2 * tiles // N_MXU


def _vpu_cycles(rows, cols, ops_per_vreg):
    return rows * cols * ops_per_vreg // (SUBLANES * 128 * 4)


def _run(gen):
    try:
        while True:
            next(gen)
    except StopIteration as stop:
        return stop.value


def _interleave(gens):
    results = [None] * len(gens)
    pending = {i: next(gen) for i, gen in enumerate(gens)}
    done = [0] * len(gens)
    ready = [0] * len(gens)
    free = [0, 0]

    def starts(i):
        m, v = pending[i]
        t_m = max(ready[i], free[0]) if m > 0 else ready[i]
        t_v = max(t_m, free[1]) if v > 0 else t_m
        return t_m, t_v

    while pending:
        allowed = [i for i in pending if i == 0 or (i - 1) not in pending or done[i] < done[i - 1]]
        i = min(allowed, key=lambda k: (starts(k)[0 if pending[k][0] > 0 else 1], k))
        (m, v), (t_m, t_v) = pending[i], starts(i)
        if m > 0:
            free[0] = t_m + m
        if v > 0:
            free[1] = t_v + v
        ready[i] = max(t_m + m, t_v + v)
        done[i] += 1
        try:
            pending[i] = next(gens[i])
        except StopIteration as stop:
            results[i] = stop.value
            del pending[i]
    return results


def _matmul_stage(lhs, w_ref, col_piece=COL_PIECE):
    rows, k = lhs.shape
    tiles = []
    for n0 in range(0, w_ref.shape[1], col_piece):
        yield _mxu_cycles(rows, k, col_piece), 0
        tiles.append(_dot(lhs, w_ref[:, n0:n0 + col_piece]))
    return jnp.concatenate(tiles, axis=1)


def _norm_stage(pre, g_ref, b_ref, rows, post=None, ops=14):
    outs, outs_b = [], []
    for r0 in range(0, rows, ROW_PIECE):
        yield 0, _vpu_cycles(ROW_PIECE, D_MODEL, ops)
        rs = slice(r0, r0 + ROW_PIECE)
        y = _layer_norm(pre(rs), g_ref[...], b_ref[...])
        outs.append(y)
        outs_b.append((y if post is None else post(y)).astype(BF16))
    return jnp.concatenate(outs, axis=0), jnp.concatenate(outs_b, axis=0)


def _dense_tail(x_ref, p_ref, y_ref, rows, mix, ln1_g, ln1_b, w_up, w_down, ln2_g, ln2_b, w_proj, w_gate):
    n = rows.stop - rows.start

    def pre1(rs):
        return DEEPNORM_ALPHA * x_ref[rows.start + rs.start:rows.start + rs.stop, :] + mix[rs]

    x1, x1b = yield from _norm_stage(pre1, ln1_g, ln1_b, n)
    hid = []
    for n0 in range(0, D_FF, COL_PIECE):
        yield _mxu_cycles(n, D_MODEL, COL_PIECE), _vpu_cycles(n, COL_PIECE, 5)
        h = _dot(x1b, w_up[:, n0:n0 + COL_PIECE])
        hid.append(jnp.square(jnp.maximum(h, 0.0)).astype(BF16))
    down = yield from _matmul_stage(jnp.concatenate(hid, axis=1), w_down)

    def pre2(rs):
        return DEEPNORM_ALPHA * x1[rs] + down[rs]

    x2, x2b = yield from _norm_stage(pre2, ln2_g, ln2_b, n)
    for n0 in range(0, D_MODEL, COL_PIECE):
        cs = slice(n0, n0 + COL_PIECE)
        yield (_mxu_cycles(n, D_MODEL, COL_PIECE) + _mxu_cycles(n, D_PLE, COL_PIECE),
               _vpu_cycles(n, COL_PIECE, 7))
        gate = jax.nn.sigmoid(_dot(x2b, w_gate[:, cs]))
        y_ref[rows, cs] = x2[:, cs] + _mm(p_ref[rows, :], w_proj[:, cs]) * gate


def _gmlp_block(x_ref, rows, pos0, hist, keep, causal, w_in, b_in, ln_g, ln_b, b_s, w_out):
    n = rows.stop - rows.start
    z, xb = [], None
    for n0 in range(0, 2 * D_MODEL, COL_PIECE):
        yield _mxu_cycles(n, D_MODEL, COL_PIECE), _vpu_cycles(n, COL_PIECE, 11)
        if xb is None:
            xb = x_ref[rows, :].astype(BF16)
        cs = slice(n0, n0 + COL_PIECE)
        z.append(jax.nn.gelu(_dot(xb, w_in[:, cs]) + b_in[:, cs], approximate=True))
    z = jnp.concatenate(z, axis=1)
    u, v = z[:, :D_MODEL], z[:, D_MODEL:]
    vn, vnb = yield from _norm_stage(lambda rs: v[rs], ln_g, ln_b, n, ops=11)
    gated = []
    for c in range(n // CHUNK):
        rs = slice(c * CHUNK, (c + 1) * CHUNK)
        heads = []
        for h in range(A_HEADS):
            if h % 2 == 0:
                yield 2 * _mxu_cycles(CHUNK, A_GROUP, A_GROUP), 2 * _vpu_cycles(CHUNK, A_GROUP, 4)
            cs = slice(h * A_GROUP, (h + 1) * A_GROUP)
            s = _dot(causal[h], vnb[rs, cs]) + b_s[h]
            heads.append((u[rs, cs] * s).astype(BF16))
        gated.append(jnp.concatenate(heads, axis=1))
    mix = yield from _matmul_stage(jnp.concatenate(gated, axis=0), w_out)
    return mix, vn[n - CHUNK:, :]


def _pool_block(x_ref, rows, pos0, hist, keep, w_in, w_grp, scale, w_out):
    n = rows.stop - rows.start
    h, xb = [], None
    for n0 in range(0, D_MODEL, COL_PIECE):
        yield _mxu_cycles(n, D_MODEL, COL_PIECE), 0
        if xb is None:
            xb = x_ref[rows, :].astype(BF16)
        h.append(_dot(xb, w_in[:, n0:n0 + COL_PIECE]))
    h = jnp.concatenate(h, axis=1)
    pos1 = (pos0 + 1 + lax.broadcasted_iota(jnp.int32, (n, 1), 0)).astype(F32)
    mixed = []
    for g, w in enumerate(POOL_WINDOWS):
        assert w & (w - 1) == 0 and w <= POOL_HALO
        yield _mxu_cycles(n, B_GROUP, B_GROUP), _vpu_cycles(n, B_GROUP, 4 * w.bit_length() + 4)
        cs = slice(g * B_GROUP, (g + 1) * B_GROUP)
        hg = h[:, cs]
        keep(cs, hg[n - POOL_HALO:, :])
        acc = jnp.concatenate([hist(cs), hg], axis=0)
        step = 1
        while step < w:
            acc = acc + pltpu.roll(acc, step, axis=0)
            step *= 2
        d = acc[POOL_HALO:, :] / jnp.minimum(float(w), pos1) - hg
        mixed.append((_mm(d, w_grp[g]) * scale[:, cs]).astype(BF16))
    mix = yield from _matmul_stage(jnp.concatenate(mixed, axis=1), w_out)
    return mix, None


def _conformer_block(x_ref, rows, pos0, hist, keep, w_in, b_in, w_dw, b_dw, ln_g, ln_b, w_out):
    n = rows.stop - rows.start
    y, xb = [], None
    for n0 in range(0, D_MODEL, MIX_PIECE):
        yield 2 * _mxu_cycles(n, D_MODEL, MIX_PIECE), _vpu_cycles(n, MIX_PIECE, 7)
        if xb is None:
            xb = x_ref[rows, :].astype(BF16)
        cs = slice(n0, n0 + MIX_PIECE)
        cs2 = slice(D_MODEL + n0, D_MODEL + n0 + MIX_PIECE)
        a = _dot(xb, w_in[:, cs]) + b_in[:, cs]
        g = a * jax.nn.sigmoid(_dot(xb, w_in[:, cs2]) + b_in[:, cs2])
        keep(cs, g[n - CONV_HALO:, :])
        ext = jnp.concatenate([hist(cs), g], axis=0)
        acc = None
        for r in range(SUBLANES):
            taps = (CONV_K - 1 - r) // SUBLANES + 1
            yield 0, _vpu_cycles(n, MIX_PIECE, 2 * taps + 3)
            ext_r = ext if r == 0 else pltpu.roll(ext, r, axis=0)
            for a_ in range(taps):
                k = CONV_K - 1 - (SUBLANES * a_ + r)
                start = CONV_HALO - SUBLANES * a_
                term = ext_r[start:start + n, :] * w_dw[k:k + 1, cs]
                acc = term if acc is None else acc + term
        y.append(acc)
    y = jnp.concatenate(y, axis=1)
    _, act = yield from _norm_stage(lambda rs: y[rs] + b_dw[...], ln_g, ln_b, n,
                                    post=lambda yn: yn * jax.nn.sigmoid(yn), ops=16)
    mix = yield from _matmul_stage(act, w_out)
    return mix, None


def _sconv_block(x_ref, rows, pos0, hist, keep, w_in, w_conv, w_out):
    n = rows.stop - rows.start
    gated, xb = [], None
    for n0 in range(0, D_MODEL, MIX_PIECE):
        yield 3 * _mxu_cycles(n, D_MODEL, MIX_PIECE), _vpu_cycles(n, MIX_PIECE, 14)
        if xb is None:
            xb = x_ref[rows, :].astype(BF16)
        cs = slice(n0, n0 + MIX_PIECE)
        bg, cg, hh = (_dot(xb, w_in[:, i * D_MODEL + n0:i * D_MODEL + n0 + MIX_PIECE]) for i in range(3))
        q = cg * hh
        keep(cs, q[n - SCONV_HALO:, :])
        ext = jnp.concatenate([hist(cs), q], axis=0)
        y = q * w_conv[SCONV_K - 1:SCONV_K, cs]
        for s in range(1, SCONV_K):
            k = SCONV_K - 1 - s
            y = y + pltpu.roll(ext, s, axis=0)[SCONV_HALO:, :] * w_conv[k:k + 1, cs]
        gated.append((bg * y).astype(BF16))
    mix = yield from _matmul_stage(jnp.concatenate(gated, axis=1), w_out)
    return mix, None


def _prompt_kernel(block_fn, n_mixer, sub, x_ref, p_ref, *rest):
    mixer, dense = rest[:n_mixer], rest[n_mixer:n_mixer + N_DENSE]
    y_ref, aux_ref, *scratch = rest[n_mixer + N_DENSE:]
    tile = x_ref.shape[0]
    n_blocks = tile // sub
    halo = scratch[0] if scratch else None
    if halo is not None:
        @pl.when(pl.program_id(1) == 0)
        def _():
            halo[...] = jnp.zeros_like(halo)

    if block_fn is _gmlp_block:
        w_s, mixer = mixer[0], mixer[1:]
        row = lax.broadcasted_iota(jnp.int32, (CHUNK, CHUNK), 0)
        col = lax.broadcasted_iota(jnp.int32, (CHUNK, CHUNK), 1)
        mixer = ([jnp.where(row >= col, w_s[h], 0.0).astype(BF16) for h in range(A_HEADS)],) + tuple(mixer)

    handoff = [dict() for _ in range(n_blocks + 1)]

    def block(k):
        rows = slice(k * sub, (k + 1) * sub)
        pos0 = pl.program_id(1) * tile + k * sub

        def hist(cs):
            return halo[:, cs] if k == 0 else handoff[k][cs.start]

        def keep(cs, value):
            if k == n_blocks - 1:
                halo[:, cs] = value
            else:
                handoff[k + 1][cs.start] = value

        mix, tail = yield from block_fn(x_ref, rows, pos0, hist, keep, *mixer)
        yield from _dense_tail(x_ref, p_ref, y_ref, rows, mix, *dense)
        return tail

    tails = _interleave([block(k) for k in range(n_blocks)])

    @pl.when(pl.program_id(1) == pl.num_programs(1) - 1)
    def _():
        if halo is not None:
            aux_ref[...] = halo[halo.shape[0] - aux_ref.shape[0]:, :]
        else:
            aux_ref[...] = tails[-1]


def _hist_row(st_ref, k, cols=slice(None)):
    return st_ref[:, k, cols]


def _sample_tail(x_ref, p_ref, y_ref, mix, dense):
    _run(_dense_tail(x_ref, p_ref, y_ref, slice(0, x_ref.shape[0]), mix, *dense))


def _sample_gmlp_kernel(x_ref, p_ref, w_in, b_in, ln_g, ln_b, w_s0, b_s0, w_out, *rest):
    dense, (y_ref, chunkv_ref) = rest[:N_DENSE], rest[N_DENSE:]
    z = jax.nn.gelu(_mm(x_ref[...], w_in[...]) + b_in[...], approximate=True)
    u = z[:, :D_MODEL]
    vn = _layer_norm(z[:, D_MODEL:], ln_g[...], ln_b[...])
    s = jnp.concatenate(
        [vn[:, h * A_GROUP:(h + 1) * A_GROUP] * w_s0[h] + b_s0[h] for h in range(A_HEADS)], axis=1)
    chunkv_ref[...] = vn
    _sample_tail(x_ref, p_ref, y_ref, _mm(u * s, w_out[...]), dense)


def _sample_pool_kernel(x_ref, p_ref, st_ref, w_in, w_grp, scale, w_out, *rest):
    dense, (y_ref, h_ref) = rest[:N_DENSE], rest[N_DENSE:]
    h = _mm(x_ref[...], w_in[...])
    mixed = []
    for g, w in enumerate(POOL_WINDOWS):
        cols = slice(g * B_GROUP, (g + 1) * B_GROUP)
        acc = h[:, cols]
        for k in range(POOL_BUF - (w - 1), POOL_BUF):
            acc = acc + _hist_row(st_ref, k, cols)
        d = acc / float(min(w, PAST_LEN + 1)) - h[:, cols]
        mixed.append(_mm(d, w_grp[g]))
    h_ref[...] = h
    mix = _mm(jnp.concatenate(mixed, axis=1) * scale[...], w_out[...])
    _sample_tail(x_ref, p_ref, y_ref, mix, dense)


def _sample_conformer_kernel(x_ref, p_ref, st_ref, w_in, b_in, w_dw, b_dw, ln_g, ln_b, w_out, *rest):
    dense, (y_ref, g_ref) = rest[:N_DENSE], rest[N_DENSE:]
    z = _mm(x_ref[...], w_in[...]) + b_in[...]
    g = z[:, :D_MODEL] * jax.nn.sigmoid(z[:, D_MODEL:])
    y = g * w_dw[CONV_K - 1:CONV_K, :]
    for k in range(CONV_K - 1):
        y = y + _hist_row(st_ref, k) * w_dw[k:k + 1, :]
    g_ref[...] = g
    yn = _layer_norm(y + b_dw[...], ln_g[...], ln_b[...])
    mix = _mm(yn * jax.nn.sigmoid(yn), w_out[...])
    _sample_tail(x_ref, p_ref, y_ref, mix, dense)


def _sample_sconv_kernel(x_ref, p_ref, st_ref, w_in, w_conv, w_out, *rest):
    dense, (y_ref, q_ref) = rest[:N_DENSE], rest[N_DENSE:]
    z = _mm(x_ref[...], w_in[...])
    bg = z[:, :D_MODEL]
    q = z[:, D_MODEL:2 * D_MODEL] * z[:, 2 * D_MODEL:]
    y = q * w_conv[SCONV_K - 1:SCONV_K, :]
    for k in range(SCONV_K - 1):
        y = y + _hist_row(st_ref, k) * w_conv[k:k + 1, :]
    q_ref[...] = q
    _sample_tail(x_ref, p_ref, y_ref, _mm(bg * y, w_out[...]), dense)


def _resident(arr, lead=None):
    if lead is None:
        shape, idx = arr.shape, (0,) * arr.ndim
    else:
        shape, idx = (None,) + arr.shape[1:], (lead,) + (0,) * (arr.ndim - 1)
    return pl.BlockSpec(shape, lambda *_: idx, pipeline_mode=pl.Buffered(1))


def _smem():
    return pl.BlockSpec(memory_space=pltpu.SMEM)


def _dense_operands(layer, prm):
    arrs = [prm['ln1_g'], prm['ln1_b'], prm['mlp_w_up'], prm['mlp_w_down'],
            prm['ln2_g'], prm['ln2_b'], prm['ple_w_proj'], prm['ple_w_gate']]
    assert len(arrs) == N_DENSE
    return arrs, [_resident(a, lead=layer) for a in arrs]


def _prompt_layer(layer, x, p, prm):
    batch, seq, _ = x.shape
    tile, sub = PROMPT_TILE, PROMPT_SUB
    assert seq % tile == 0 and tile % sub == 0 and sub % CHUNK == 0 and sub >= CONV_HALO
    kind = layer % 4
    if kind == 0:
        block_fn, aux_rows, halo_rows = _gmlp_block, CHUNK, None
        mixer = [prm['a_w_s'], prm['a_w_in'], prm['a_b_in'], prm['a_ln_g'], prm['a_ln_b'],
                 prm['a_b_s3'], prm['a_w_out']]
    elif kind == 1:
        block_fn, aux_rows, halo_rows = _pool_block, POOL_BUF, POOL_HALO
        mixer = [prm['b_w_in'], prm['b_w_grp'], prm['b_scale'], prm['b_w_out']]
    elif kind == 2:
        block_fn, aux_rows, halo_rows = _conformer_block, CONV_K - 1, CONV_HALO
        mixer = [prm['c_w_in'], prm['c_b_in'], prm['c_w_dw'], prm['c_b_dw'],
                 prm['c_ln_g'], prm['c_ln_b'], prm['c_w_out']]
    else:
        block_fn, aux_rows, halo_rows = _sconv_block, SCONV_K - 1, SCONV_HALO
        mixer = [prm['d_w_in'], prm['d_w_conv'], prm['d_w_out']]
    dense, dense_specs = _dense_operands(layer, prm)
    in_specs = [
        pl.BlockSpec((None, tile, D_MODEL), lambda b, t: (b, t, 0)),
        pl.BlockSpec((None, None, tile, D_PLE), lambda b, t: (layer, b, t, 0)),
    ] + [_resident(a) for a in mixer] + dense_specs
    out_shape = (jax.ShapeDtypeStruct(x.shape, F32),
                 jax.ShapeDtypeStruct((batch, aux_rows, D_MODEL), F32))
    out_specs = (pl.BlockSpec((None, tile, D_MODEL), lambda b, t: (b, t, 0)),
                 pl.BlockSpec((None, aux_rows, D_MODEL), lambda b, t: (b, 0, 0)))
    scratch = [] if halo_rows is None else [pltpu.VMEM((halo_rows, D_MODEL), F32)]
    return pl.pallas_call(
        functools.partial(_prompt_kernel, block_fn, len(mixer), sub),
        out_shape=out_shape,
        grid=(batch, seq // tile),
        in_specs=in_specs,
        out_specs=out_specs,
        scratch_shapes=scratch,
        compiler_params=pltpu.CompilerParams(
            dimension_semantics=("arbitrary", "arbitrary"),
            vmem_limit_bytes=VMEM_LIMIT_BYTES),
        name=f"prompt_layer{layer}",
    )(x, p, *mixer, *dense)


def _sample_layer(layer, x, p, state, prm):
    rows = x.shape[0]
    kind = layer % 4
    if kind == 0:
        body = _sample_gmlp_kernel
        mixer = [prm['a_w_in'], prm['a_b_in'], prm['a_ln_g'], prm['a_ln_b'],
                 prm['a_w_s00'], prm['a_b_s0'], prm['a_w_out']]
        mixer_specs = [_resident(a) for a in mixer[:4]] + [_smem(), _smem(), _resident(mixer[6])]
    else:
        if kind == 1:
            body = _sample_pool_kernel
            mixer = [state, prm['b_w_in'], prm['b_w_grp'], prm['b_scale'], prm['b_w_out']]
        elif kind == 2:
            body = _sample_conformer_kernel
            mixer = [state, prm['c_w_in'], prm['c_b_in'], prm['c_w_dw'], prm['c_b_dw'],
                     prm['c_ln_g'], prm['c_ln_b'], prm['c_w_out']]
        else:
            body = _sample_sconv_kernel
            mixer = [state, prm['d_w_in'], prm['d_w_conv'], prm['d_w_out']]
        mixer_specs = [_resident(a) for a in mixer]
    dense, dense_specs = _dense_operands(layer, prm)
    in_specs = [_resident(x), _resident(p, lead=layer)] + mixer_specs + dense_specs
    out_shape = (jax.ShapeDtypeStruct((rows, D_MODEL), F32),) * 2
    out_specs = (pl.BlockSpec((rows, D_MODEL), lambda i: (0, 0)),) * 2
    return pl.pallas_call(
        body,
        out_shape=out_shape,
        grid=(1,),
        in_specs=in_specs,
        out_specs=out_specs,
        compiler_params=pltpu.CompilerParams(
            dimension_semantics=("arbitrary",),
            vmem_limit_bytes=VMEM_LIMIT_BYTES),
        name=f"sample_layer{layer}",
    )(x, p, *mixer, *dense)


def _row(v):
    return v.reshape(1, -1)


def kernel(x_prompt, x_sample, state_pool, state_conv, state_shortconv, p_prompt, p_sample, a_w_in, a_b_in, a_ln_g, a_ln_b, a_w_s, a_b_s, a_w_out, b_w_in, b_w_grp, b_scale, b_w_out, c_w_in, c_b_in, c_w_dw, c_b_dw, c_ln_g, c_ln_b, c_w_out, d_w_in, d_w_conv, d_w_out, ln1_g, ln1_b, ln2_g, ln2_b, mlp_w_up, mlp_w_down, ple_w_proj, ple_w_gate):
    rows = x_sample.shape[0]
    depth = mlp_w_up.shape[0]
    prm = {
        'a_w_in': a_w_in.astype(BF16), 'a_w_out': a_w_out.astype(BF16),
        'b_w_in': b_w_in.astype(BF16), 'b_w_grp': b_w_grp.astype(BF16), 'b_w_out': b_w_out.astype(BF16),
        'c_w_in': c_w_in.astype(BF16), 'c_w_out': c_w_out.astype(BF16),
        'd_w_in': d_w_in.astype(BF16), 'd_w_out': d_w_out.astype(BF16),
        'mlp_w_up': mlp_w_up.astype(BF16), 'mlp_w_down': mlp_w_down.astype(BF16),
        'ple_w_proj': ple_w_proj.astype(BF16), 'ple_w_gate': ple_w_gate.astype(BF16),
        'a_b_in': _row(a_b_in), 'a_ln_g': _row(a_ln_g), 'a_ln_b': _row(a_ln_b),
        'a_w_s': a_w_s, 'a_b_s3': a_b_s[:, :, None],
        'a_w_s00': a_w_s[:, 0, 0], 'a_b_s0': a_b_s[:, 0],
        'b_scale': _row(b_scale),
        'c_b_in': _row(c_b_in), 'c_w_dw': c_w_dw, 'c_b_dw': _row(c_b_dw),
        'c_ln_g': _row(c_ln_g), 'c_ln_b': _row(c_ln_b),
        'd_w_conv': d_w_conv,
        'ln1_g': ln1_g[:, None, :], 'ln1_b': ln1_b[:, None, :],
        'ln2_g': ln2_g[:, None, :], 'ln2_b': ln2_b[:, None, :],
    }

    aux_prompt = []
    x = x_prompt
    for layer in range(depth):
        x, aux = _prompt_layer(layer, x, p_prompt, prm)
        aux_prompt.append(aux)
    y_prompt = x

    states = [None, state_pool, state_conv, state_shortconv]
    p_s = p_sample.reshape(depth, rows, D_PLE)
    aux_sample = []
    x = x_sample.reshape(rows, D_MODEL)
    for layer in range(depth):
        x, aux = _sample_layer(layer, x, p_s, states[layer % 4], prm)
        aux_sample.append(aux[:, None, :])
    y_sample = x[:, None, :]

    def shifted(state, new_row):
        return jnp.concatenate([state[:, 1:, :], new_row], axis=1)

    return (y_prompt, y_sample, aux_prompt[0], aux_sample[0],
            aux_prompt[1], shifted(state_pool, aux_sample[1]),
            aux_prompt[2], shifted(state_conv, aux_sample[2]),
            aux_prompt[3], shifted(state_shortconv, aux_sample[3]))
```

```python
import functools

import jax
import jax.numpy as jnp
from jax import lax
from jax.experimental import pallas as pl
from jax.experimental.pallas import tpu as pltpu

D_MODEL = 1024
D_PLE = 256
D_FF = 4 * D_MODEL
DEPTH = 4
PAST_LEN = 16384
LN_EPS = 1e-5
DEEPNORM_ALPHA = (2 * DEPTH) ** 0.25
CHUNK = 128
A_GROUP = 128
A_HEADS = D_MODEL // A_GROUP
POOL_WINDOWS = (2, 4, 8, 16)
B_GROUP = D_MODEL // len(POOL_WINDOWS)
POOL_BUF = max(POOL_WINDOWS) - 1
CONV_K = 31
SCONV_K = 3

SUBLANES = 8
MXU_DIM = 256
N_MXU = 2
V7X_VMEM_BYTES = 64 * 1024 * 1024
VMEM_LIMIT_BYTES = V7X_VMEM_BYTES - 3 * 1024 * 1024

PROMPT_TILE = 1024
PROMPT_SUB = 256
ROW_PIECE = 32
COL_PIECE = 512
MIX_PIECE = 256
POOL_HALO = 16
CONV_HALO = 32
SCONV_HALO = 8
N_DENSE = 8

F32 = jnp.float32
BF16 = jnp.bfloat16


def _dot(a, b):
    return jnp.dot(a, b, preferred_element_type=F32)


def _mm(a, w):
    return _dot(a.astype(BF16), w)


def _layer_norm(x, g, b):
    mu = jnp.mean(x, axis=-1, keepdims=True)
    xc = x - mu
    var = jnp.mean(xc * xc, axis=-1, keepdims=True)
    return xc * lax.rsqrt(var + LN_EPS) * g + b


def _mxu_cycles(rows, k, n):
    tiles = -(-k // MXU_DIM) * -(-n // MXU_DIM)
    return rows // 2 * tiles // N_MXU


def _vpu_cycles(rows, cols, ops_per_vreg):
    return rows * cols * ops_per_vreg // (SUBLANES * 128 * 4)


def _run(gen):
    try:
        while True:
            next(gen)
    except StopIteration as stop:
        return stop.value


def _interleave(gens):
    results = [None] * len(gens)
    pending = {i: next(gen) for i, gen in enumerate(gens)}
    done = [0] * len(gens)
    ready = [0] * len(gens)
    free = [0, 0]

    def starts(i):
        m, v = pending[i]
        t_m = max(ready[i], free[0]) if m > 0 else ready[i]
        t_v = max(t_m, free[1]) if v > 0 else t_m
        return t_m, t_v

    while pending:
        allowed = [i for i in pending if i == 0 or (i - 1) not in pending or done[i] < done[i - 1]]
        i = min(allowed, key=lambda k: (starts(k)[0 if pending[k][0] > 0 else 1], k))
        (m, v), (t_m, t_v) = pending[i], starts(i)
        if m > 0:
            free[0] = t_m + m
        if v > 0:
            free[1] = t_v + v
        ready[i] = max(t_m + m, t_v + v)
        done[i] += 1
        try:
            pending[i] = next(gens[i])
        except StopIteration as stop:
            results[i] = stop.value
            del pending[i]
    return results


def _matmul_stage(lhs, w_ref, col_piece=COL_PIECE):
    rows, k = lhs.shape
    tiles = []
    for n0 in range(0, w_ref.shape[1], col_piece):
        yield _mxu_cycles(rows, k, col_piece), 0
        tiles.append(_dot(lhs, w_ref[:, n0:n0 + col_piece]))
    return jnp.concatenate(tiles, axis=1)


def _norm_stage(pre, g_ref, b_ref, rows, post=None, ops=14):
    outs, outs_b = [], []
    for r0 in range(0, rows, ROW_PIECE):
        yield 0, _vpu_cycles(ROW_PIECE, D_MODEL, ops)
        rs = slice(r0, r0 + ROW_PIECE)
        y = _layer_norm(pre(rs), g_ref[...], b_ref[...])
        outs.append(y)
        outs_b.append((y if post is None else post(y)).astype(BF16))
    return jnp.concatenate(outs, axis=0), jnp.concatenate(outs_b, axis=0)


def _dense_tail(x_ref, p_ref, y_ref, rows, mix, ln1_g, ln1_b, w_up, w_down, ln2_g, ln2_b, w_proj, w_gate):
    n = rows.stop - rows.start

    def pre1(rs):
        return DEEPNORM_ALPHA * x_ref[rows.start + rs.start:rows.start + rs.stop, :] + mix[rs]

    x1, x1b = yield from _norm_stage(pre1, ln1_g, ln1_b, n)
    hid = []
    for n0 in range(0, D_FF, COL_PIECE):
        yield _mxu_cycles(n, D_MODEL, COL_PIECE), _vpu_cycles(n, COL_PIECE, 5)
        h = _dot(x1b, w_up[:, n0:n0 + COL_PIECE])
        hid.append(jnp.square(jnp.maximum(h, 0.0)).astype(BF16))
    down = yield from _matmul_stage(jnp.concatenate(hid, axis=1), w_down)

    def pre2(rs):
        return DEEPNORM_ALPHA * x1[rs] + down[rs]

    x2, x2b = yield from _norm_stage(pre2, ln2_g, ln2_b, n)
    for n0 in range(0, D_MODEL, COL_PIECE):
        cs = slice(n0, n0 + COL_PIECE)
        yield (_mxu_cycles(n, D_MODEL, COL_PIECE) + _mxu_cycles(n, D_PLE, COL_PIECE),
               _vpu_cycles(n, COL_PIECE, 7))
        gate = jax.nn.sigmoid(_dot(x2b, w_gate[:, cs]))
        y_ref[rows, cs] = x2[:, cs] + _mm(p_ref[rows, :], w_proj[:, cs]) * gate


def _gmlp_block(x_ref, rows, pos0, hist, keep, causal, w_in, b_in, ln_g, ln_b, b_s, w_out):
    n = rows.stop - rows.start
    z, xb = [], None
    for n0 in range(0, 2 * D_MODEL, COL_PIECE):
        yield _mxu_cycles(n, D_MODEL, COL_PIECE), _vpu_cycles(n, COL_PIECE, 11)
        if xb is None:
            xb = x_ref[rows, :].astype(BF16)
        cs = slice(n0, n0 + COL_PIECE)
        z.append(jax.nn.gelu(_dot(xb, w_in[:, cs]) + b_in[:, cs], approximate=True))
    z = jnp.concatenate(z, axis=1)
    u, v = z[:, :D_MODEL], z[:, D_MODEL:]
    vn, vnb = yield from _norm_stage(lambda rs: v[rs], ln_g, ln_b, n, ops=11)
    gated = []
    for c in range(n // CHUNK):
        rs = slice(c * CHUNK, (c + 1) * CHUNK)
        heads = []
        for h in range(A_HEADS):
            if h % 2 == 0:
                yield 2 * _mxu_cycles(CHUNK, A_GROUP, A_GROUP), 2 * _vpu_cycles(CHUNK, A_GROUP, 4)
            cs = slice(h * A_GROUP, (h + 1) * A_GROUP)
            s = _dot(causal[h], vnb[rs, cs]) + b_s[h]
            heads.append((u[rs, cs] * s).astype(BF16))
        gated.append(jnp.concatenate(heads, axis=1))
    mix = yield from _matmul_stage(jnp.concatenate(gated, axis=0), w_out)
    return mix, vn[n - CHUNK:, :]


def _pool_block(x_ref, rows, pos0, hist, keep, w_in, w_grp, scale, w_out):
    n = rows.stop - rows.start
    h, xb = [], None
    for n0 in range(0, D_MODEL, COL_PIECE):
        yield _mxu_cycles(n, D_MODEL, COL_PIECE), 0
        if xb is None:
            xb = x_ref[rows, :].astype(BF16)
        h.append(_dot(xb, w_in[:, n0:n0 + COL_PIECE]))
    h = jnp.concatenate(h, axis=1)
    pos1 = (pos0 + 1 + lax.broadcasted_iota(jnp.int32, (n, 1), 0)).astype(F32)
    mixed = []
    for g, w in enumerate(POOL_WINDOWS):
        assert w & (w - 1) == 0 and w <= POOL_HALO
        yield _mxu_cycles(n, B_GROUP, B_GROUP), _vpu_cycles(n, B_GROUP, 4 * w.bit_length() + 4)
        cs = slice(g * B_GROUP, (g + 1) * B_GROUP)
        hg = h[:, cs]
        keep(cs, hg[n - POOL_HALO:, :])
        acc = jnp.concatenate([hist(cs), hg], axis=0)
        step = 1
        while step < w:
            acc = acc + pltpu.roll(acc, step, axis=0)
            step *= 2
        d = acc[POOL_HALO:, :] / jnp.minimum(float(w), pos1) - hg
        mixed.append((_mm(d, w_grp[g]) * scale[:, cs]).astype(BF16))
    mix = yield from _matmul_stage(jnp.concatenate(mixed, axis=1), w_out)
    return mix, None


def _conformer_block(x_ref, rows, pos0, hist, keep, w_in, b_in, w_dw, b_dw, ln_g, ln_b, w_out):
    n = rows.stop - rows.start
    y, xb = [], None
    for n0 in range(0, D_MODEL, MIX_PIECE):
        yield 2 * _mxu_cycles(n, D_MODEL, MIX_PIECE), _vpu_cycles(n, MIX_PIECE, 7)
        if xb is None:
            xb = x_ref[rows, :].astype(BF16)
        cs = slice(n0, n0 + MIX_PIECE)
        cs2 = slice(D_MODEL + n0, D_MODEL + n0 + MIX_PIECE)
        a = _dot(xb, w_in[:, cs]) + b_in[:, cs]
        g = a * jax.nn.sigmoid(_dot(xb, w_in[:, cs2]) + b_in[:, cs2])
        keep(cs, g[n - CONV_HALO:, :])
        ext = jnp.concatenate([hist(cs), g], axis=0)
        acc = None
        for r in range(SUBLANES):
            taps = (CONV_K - 1 - r) // SUBLANES + 1
            yield 0, _vpu_cycles(n, MIX_PIECE, 2 * taps + 3)
            ext_r = ext if r == 0 else pltpu.roll(ext, r, axis=0)
            for a_ in range(taps):
                k = CONV_K - 1 - (SUBLANES * a_ + r)
                start = CONV_HALO - SUBLANES * a_
                term = ext_r[start:start + n, :] * w_dw[k:k + 1, cs]
                acc = term if acc is None else acc + term
        y.append(acc)
    y = jnp.concatenate(y, axis=1)
    _, act = yield from _norm_stage(lambda rs: y[rs] + b_dw[...], ln_g, ln_b, n,
                                    post=lambda yn: yn * jax.nn.sigmoid(yn), ops=16)
    mix = yield from _matmul_stage(act, w_out)
    return mix, None


def _sconv_block(x_ref, rows, pos0, hist, keep, w_in, w_conv, w_out):
    n = rows.stop - rows.start
    gated, xb = [], None
    for n0 in range(0, D_MODEL, MIX_PIECE):
        yield 3 * _mxu_cycles(n, D_MODEL, MIX_PIECE), _vpu_cycles(n, MIX_PIECE, 14)
        if xb is None:
            xb = x_ref[rows, :].astype(BF16)
        cs = slice(n0, n0 + MIX_PIECE)
        bg, cg, hh = (_dot(xb, w_in[:, i * D_MODEL + n0:i * D_MODEL + n0 + MIX_PIECE]) for i in range(3))
        q = cg * hh
        keep(cs, q[n - SCONV_HALO:, :])
        ext = jnp.concatenate([hist(cs), q], axis=0)
        y = q * w_conv[SCONV_K - 1:SCONV_K, cs]
        for s in range(1, SCONV_K):
            k = SCONV_K - 1 - s
            y = y + pltpu.roll(ext, s, axis=0)[SCONV_HALO:, :] * w_conv[k:k + 1, cs]
        gated.append((bg * y).astype(BF16))
    mix = yield from _matmul_stage(jnp.concatenate(gated, axis=1), w_out)
    return mix, None


def _prompt_kernel(block_fn, n_mixer, sub, x_ref, p_ref, *rest):
    mixer, dense = rest[:n_mixer], rest[n_mixer:n_mixer + N_DENSE]
    y_ref, aux_ref, *scratch = rest[n_mixer + N_DENSE:]
    tile = x_ref.shape[0]
    n_blocks = tile // sub
    halo = scratch[0] if scratch else None
    if halo is not None:
        @pl.when(pl.program_id(1) == 0)
        def _():
            halo[...] = jnp.zeros_like(halo)

    if block_fn is _gmlp_block:
        w_s, mixer = mixer[0], mixer[1:]
        row = lax.broadcasted_iota(jnp.int32, (CHUNK, CHUNK), 0)
        col = lax.broadcasted_iota(jnp.int32, (CHUNK, CHUNK), 1)
        mixer = ([jnp.where(row >= col, w_s[h], 0.0).astype(BF16) for h in range(A_HEADS)],) + tuple(mixer)

    handoff = [dict() for _ in range(n_blocks + 1)]

    def block(k):
        rows = slice(k * sub, (k + 1) * sub)
        pos0 = pl.program_id(1) * tile + k * sub

        def hist(cs):
            return halo[:, cs] if k == 0 else handoff[k][cs.start]

        def keep(cs, value):
            if k == n_blocks - 1:
                halo[:, cs] = value
            else:
                handoff[k + 1][cs.start] = value

        mix, tail = yield from block_fn(x_ref, rows, pos0, hist, keep, *mixer)
        yield from _dense_tail(x_ref, p_ref, y_ref, rows, mix, *dense)
        return tail

    tails = _interleave([block(k) for k in range(n_blocks)])

    @pl.when(pl.program_id(1) == pl.num_programs(1) - 1)
    def _():
        if halo is not None:
            aux_ref[...] = halo[halo.shape[0] - aux_ref.shape[0]:, :]
        else:
            aux_ref[...] = tails[-1]


def _hist_row(st_ref, k, cols=slice(None)):
    return st_ref[:, k, cols]


def _sample_tail(x_ref, p_ref, y_ref, mix, dense):
    _run(_dense_tail(x_ref, p_ref, y_ref, slice(0, x_ref.shape[0]), mix, *dense))


def _sample_gmlp_kernel(x_ref, p_ref, w_in, b_in, ln_g, ln_b, w_s0, b_s0, w_out, *rest):
    dense, (y_ref, chunkv_ref) = rest[:N_DENSE], rest[N_DENSE:]
    z = jax.nn.gelu(_mm(x_ref[...], w_in[...]) + b_in[...], approximate=True)
    u = z[:, :D_MODEL]
    vn = _layer_norm(z[:, D_MODEL:], ln_g[...], ln_b[...])
    s = jnp.concatenate(
        [vn[:, h * A_GROUP:(h + 1) * A_GROUP] * w_s0[h] + b_s0[h] for h in range(A_HEADS)], axis=1)
    chunkv_ref[...] = vn
    _sample_tail(x_ref, p_ref, y_ref, _mm(u * s, w_out[...]), dense)


def _sample_pool_kernel(x_ref, p_ref, st_ref, w_in, w_grp, scale, w_out, *rest):
    dense, (y_ref, h_ref) = rest[:N_DENSE], rest[N_DENSE:]
    h = _mm(x_ref[...], w_in[...])
    mixed = []
    for g, w in enumerate(POOL_WINDOWS):
        cols = slice(g * B_GROUP, (g + 1) * B_GROUP)
        acc = h[:, cols]
        for k in range(POOL_BUF - (w - 1), POOL_BUF):
            acc = acc + _hist_row(st_ref, k, cols)
        d = acc / float(min(w, PAST_LEN + 1)) - h[:, cols]
        mixed.append(_mm(d, w_grp[g]))
    h_ref[...] = h
    mix = _mm(jnp.concatenate(mixed, axis=1) * scale[...], w_out[...])
    _sample_tail(x_ref, p_ref, y_ref, mix, dense)


def _sample_conformer_kernel(x_ref, p_ref, st_ref, w_in, b_in, w_dw, b_dw, ln_g, ln_b, w_out, *rest):
    dense, (y_ref, g_ref) = rest[:N_DENSE], rest[N_DENSE:]
    z = _mm(x_ref[...], w_in[...]) + b_in[...]
    g = z[:, :D_MODEL] * jax.nn.sigmoid(z[:, D_MODEL:])
    y = g * w_dw[CONV_K - 1:CONV_K, :]
    for k in range(CONV_K - 1):
        y = y + _hist_row(st_ref, k) * w_dw[k:k + 1, :]
    g_ref[...] = g
    yn = _layer_norm(y + b_dw[...], ln_g[...], ln_b[...])
    mix = _mm(yn * jax.nn.sigmoid(yn), w_out[...])
    _sample_tail(x_ref, p_ref, y_ref, mix, dense)


def _sample_sconv_kernel(x_ref, p_ref, st_ref, w_in, w_conv, w_out, *rest):
    dense, (y_ref, q_ref) = rest[:N_DENSE], rest[N_DENSE:]
    z = _mm(x_ref[...], w_in[...])
    bg = z[:, :D_MODEL]
    q = z[:, D_MODEL:2 * D_MODEL] * z[:, 2 * D_MODEL:]
    y = q * w_conv[SCONV_K - 1:SCONV_K, :]
    for k in range(SCONV_K - 1):
        y = y + _hist_row(st_ref, k) * w_conv[k:k + 1, :]
    q_ref[...] = q
    _sample_tail(x_ref, p_ref, y_ref, _mm(bg * y, w_out[...]), dense)


def _resident(arr, lead=None):
    if lead is None:
        shape, idx = arr.shape, (0,) * arr.ndim
    else:
        shape, idx = (None,) + arr.shape[1:], (lead,) + (0,) * (arr.ndim - 1)
    return pl.BlockSpec(shape, lambda *_: idx, pipeline_mode=pl.Buffered(1))


def _smem():
    return pl.BlockSpec(memory_space=pltpu.SMEM)


def _dense_operands(layer, prm):
    arrs = [prm['ln1_g'], prm['ln1_b'], prm['mlp_w_up'], prm['mlp_w_down'],
            prm['ln2_g'], prm['ln2_b'], prm['ple_w_proj'], prm['ple_w_gate']]
    assert len(arrs) == N_DENSE
    return arrs, [_resident(a, lead=layer) for a in arrs]


def _prompt_layer(layer, x, p, prm):
    batch, seq, _ = x.shape
    tile, sub = PROMPT_TILE, PROMPT_SUB
    assert seq % tile == 0 and tile % sub == 0 and sub % CHUNK == 0 and sub >= CONV_HALO
    kind = layer % 4
    if kind == 0:
        block_fn, aux_rows, halo_rows = _gmlp_block, CHUNK, None
        mixer = [prm['a_w_s'], prm['a_w_in'], prm['a_b_in'], prm['a_ln_g'], prm['a_ln_b'],
                 prm['a_b_s3'], prm['a_w_out']]
    elif kind == 1:
        block_fn, aux_rows, halo_rows = _pool_block, POOL_BUF, POOL_HALO
        mixer = [prm['b_w_in'], prm['b_w_grp'], prm['b_scale'], prm['b_w_out']]
    elif kind == 2:
        block_fn, aux_rows, halo_rows = _conformer_block, CONV_K - 1, CONV_HALO
        mixer = [prm['c_w_in'], prm['c_b_in'], prm['c_w_dw'], prm['c_b_dw'],
                 prm['c_ln_g'], prm['c_ln_b'], prm['c_w_out']]
    else:
        block_fn, aux_rows, halo_rows = _sconv_block, SCONV_K - 1, SCONV_HALO
        mixer = [prm['d_w_in'], prm['d_w_conv'], prm['d_w_out']]
    dense, dense_specs = _dense_operands(layer, prm)
    in_specs = [
        pl.BlockSpec((None, tile, D_MODEL), lambda b, t: (b, t, 0)),
        pl.BlockSpec((None, None, tile, D_PLE), lambda b, t: (layer, b, t, 0)),
    ] + [_resident(a) for a in mixer] + dense_specs
    out_shape = (jax.ShapeDtypeStruct(x.shape, F32),
                 jax.ShapeDtypeStruct((batch, aux_rows, D_MODEL), F32))
    out_specs = (pl.BlockSpec((None, tile, D_MODEL), lambda b, t: (b, t, 0)),
                 pl.BlockSpec((None, aux_rows, D_MODEL), lambda b, t: (b, 0, 0)))
    scratch = [] if halo_rows is None else [pltpu.VMEM((halo_rows, D_MODEL), F32)]
    return pl.pallas_call(
        functools.partial(_prompt_kernel, block_fn, len(mixer), sub),
        out_shape=out_shape,
        grid=(batch, seq // tile),
        in_specs=in_specs,
        out_specs=out_specs,
        scratch_shapes=scratch,
        compiler_params=pltpu.CompilerParams(
            dimension_semantics=("arbitrary", "arbitrary"),
            vmem_limit_bytes=VMEM_LIMIT_BYTES),
        name=f"prompt_layer{layer}",
    )(x, p, *mixer, *dense)


def _sample_layer(layer, x, p, state, prm):
    rows = x.shape[0]
    kind = layer % 4
    if kind == 0:
        body = _sample_gmlp_kernel
        mixer = [prm['a_w_in'], prm['a_b_in'], prm['a_ln_g'], prm['a_ln_b'],
                 prm['a_w_s00'], prm['a_b_s0'], prm['a_w_out']]
        mixer_specs = [_resident(a) for a in mixer[:4]] + [_smem(), _smem(), _resident(mixer[6])]
    else:
        if kind == 1:
            body = _sample_pool_kernel
            mixer = [state, prm['b_w_in'], prm['b_w_grp'], prm['b_scale'], prm['b_w_out']]
        elif kind == 2:
            body = _sample_conformer_kernel
            mixer = [state, prm['c_w_in'], prm['c_b_in'], prm['c_w_dw'], prm['c_b_dw'],
                     prm['c_ln_g'], prm['c_ln_b'], prm['c_w_out']]
        else:
            body = _sample_sconv_kernel
            mixer = [state, prm['d_w_in'], prm['d_w_conv'], prm['d_w_out']]
        mixer_specs = [_resident(a) for a in mixer]
    dense, dense_specs = _dense_operands(layer, prm)
    in_specs = [_resident(x), _resident(p, lead=layer)] + mixer_specs + dense_specs
    out_shape = (jax.ShapeDtypeStruct((rows, D_MODEL), F32),) * 2
    out_specs = (pl.BlockSpec((rows, D_MODEL), lambda i: (0, 0)),) * 2
    return pl.pallas_call(
        body,
        out_shape=out_shape,
        grid=(1,),
        in_specs=in_specs,
        out_specs=out_specs,
        compiler_params=pltpu.CompilerParams(
            dimension_semantics=("arbitrary",),
            vmem_limit_bytes=VMEM_LIMIT_BYTES),
        name=f"sample_layer{layer}",
    )(x, p, *mixer, *dense)


def _row(v):
    return v.reshape(1, -1)


def kernel(x_prompt, x_sample, state_pool, state_conv, state_shortconv, p_prompt, p_sample, a_w_in, a_b_in, a_ln_g, a_ln_b, a_w_s, a_b_s, a_w_out, b_w_in, b_w_grp, b_scale, b_w_out, c_w_in, c_b_in, c_w_dw, c_b_dw, c_ln_g, c_ln_b, c_w_out, d_w_in, d_w_conv, d_w_out, ln1_g, ln1_b, ln2_g, ln2_b, mlp_w_up, mlp_w_down, ple_w_proj, ple_w_gate):
    rows = x_sample.shape[0]
    depth = mlp_w_up.shape[0]
    prm = {
        'a_w_in': a_w_in.astype(BF16), 'a_w_out': a_w_out.astype(BF16),
        'b_w_in': b_w_in.astype(BF16), 'b_w_grp': b_w_grp.astype(BF16), 'b_w_out': b_w_out.astype(BF16),
        'c_w_in': c_w_in.astype(BF16), 'c_w_out': c_w_out.astype(BF16),
        'd_w_in': d_w_in.astype(BF16), 'd_w_out': d_w_out.astype(BF16),
        'mlp_w_up': mlp_w_up.astype(BF16), 'mlp_w_down': mlp_w_down.astype(BF16),
        'ple_w_proj': ple_w_proj.astype(BF16), 'ple_w_gate': ple_w_gate.astype(BF16),
        'a_b_in': _row(a_b_in), 'a_ln_g': _row(a_ln_g), 'a_ln_b': _row(a_ln_b),
        'a_w_s': a_w_s, 'a_b_s3': a_b_s[:, :, None],
        'a_w_s00': a_w_s[:, 0, 0], 'a_b_s0': a_b_s[:, 0],
        'b_scale': _row(b_scale),
        'c_b_in': _row(c_b_in), 'c_w_dw': c_w_dw, 'c_b_dw': _row(c_b_dw),
        'c_ln_g': _row(c_ln_g), 'c_ln_b': _row(c_ln_b),
        'd_w_conv': d_w_conv,
        'ln1_g': ln1_g[:, None, :], 'ln1_b': ln1_b[:, None, :],
        'ln2_g': ln2_g[:, None, :], 'ln2_b': ln2_b[:, None, :],
    }

    aux_prompt = []
    x = x_prompt
    for layer in range(depth):
        x, aux = _prompt_layer(layer, x, p_prompt, prm)
        aux_prompt.append(aux)
    y_prompt = x

    states = [None, state_pool, state_conv, state_shortconv]
    p_s = p_sample.reshape(depth, rows, D_PLE)
    aux_sample = []
    x = x_sample.reshape(rows, D_MODEL)
    for layer in range(depth):
        x, aux = _sample_layer(layer, x, p_s, states[layer % 4], prm)
        aux_sample.append(aux[:, None, :])
    y_sample = x[:, None, :]

    def shifted(state, new_row):
        return jnp.concatenate([state[:, 1:, :], new_row], axis=1)

    return (y_prompt, y_sample, aux_prompt[0], aux_sample[0],
            aux_prompt[1], shifted(state_pool, aux_sample[1]),
            aux_prompt[2], shifted(state_conv, aux_sample[2]),
            aux_prompt[3], shifted(state_shortconv, aux_sample[3]))
```

```python
import functools

import jax
import jax.numpy as jnp
from jax import lax
from jax.experimental import pallas as pl
from jax.experimental.pallas import tpu as pltpu

D_MODEL = 1024
D_PLE = 256
D_FF = 4 * D_MODEL
DEPTH = 4
PAST_LEN = 16384
LN_EPS = 1e-5
DEEPNORM_ALPHA = (2 * DEPTH) ** 0.25
CHUNK = 128
A_GROUP = 128
A_HEADS = D_MODEL // A_GROUP
POOL_WINDOWS = (2, 4, 8, 16)
B_GROUP = D_MODEL // len(POOL_WINDOWS)
POOL_BUF = max(POOL_WINDOWS) - 1
CONV_K = 31
SCONV_K = 3

SUBLANES = 8
MXU_DIM = 256
N_MXU = 2
V7X_VMEM_BYTES = 64 * 1024 * 1024
VMEM_LIMIT_BYTES = V7X_VMEM_BYTES - 3 * 1024 * 1024

PROMPT_TILES = (512, 1024, 1024, 1024)
PROMPT_SUB = 256
ROW_PIECE = 32
COL_PIECE = 512
MIX_PIECE = 256
POOL_HALO = 16
CONV_HALO = 32
SCONV_HALO = 8
N_DENSE = 8

F32 = jnp.float32
BF16 = jnp.bfloat16


def _dot(a, b):
    return jnp.dot(a, b, preferred_element_type=F32)


def _mm(a, w):
    return _dot(a.astype(BF16), w)


def _layer_norm(x, g, b):
    mu = jnp.mean(x, axis=-1, keepdims=True)
    xc = x - mu
    var = jnp.mean(xc * xc, axis=-1, keepdims=True)
    return xc * lax.rsqrt(var + LN_EPS) * g + b


def _mxu_cycles(rows, k, n):
    tiles = -(-k // MXU_DIM) * -(-n // MXU_DIM)
    return rows // 2 * tiles // N_MXU


def _vpu_cycles(rows, cols, ops_per_vreg):
    return rows * cols * ops_per_vreg // (SUBLANES * 128 * 4)


def _run(gen):
    try:
        while True:
            next(gen)
    except StopIteration as stop:
        return stop.value


def _interleave(gens):
    results = [None] * len(gens)
    pending = {i: next(gen) for i, gen in enumerate(gens)}
    done = [0] * len(gens)
    ready = [0] * len(gens)
    free = [0, 0]

    def starts(i):
        m, v = pending[i]
        t_m = max(ready[i], free[0]) if m > 0 else ready[i]
        t_v = max(t_m, free[1]) if v > 0 else t_m
        return t_m, t_v

    while pending:
        allowed = [i for i in pending if i == 0 or (i - 1) not in pending or done[i] < done[i - 1]]
        i = min(allowed, key=lambda k: (starts(k)[0 if pending[k][0] > 0 else 1], k))
        (m, v), (t_m, t_v) = pending[i], starts(i)
        if m > 0:
            free[0] = t_m + m
        if v > 0:
            free[1] = t_v + v
        ready[i] = max(t_m + m, t_v + v)
        done[i] += 1
        try:
            pending[i] = next(gens[i])
        except StopIteration as stop:
            results[i] = stop.value
            del pending[i]
    return results


def _matmul_stage(lhs, w_ref, col_piece=COL_PIECE):
    rows, k = lhs.shape
    tiles = []
    for n0 in range(0, w_ref.shape[1], col_piece):
        yield _mxu_cycles(rows, k, col_piece), 0
        tiles.append(_dot(lhs, w_ref[:, n0:n0 + col_piece]))
    return jnp.concatenate(tiles, axis=1)


def _norm_stage(pre, g_ref, b_ref, rows, post=None, ops=14):
    outs, outs_b = [], []
    for r0 in range(0, rows, ROW_PIECE):
        yield 0, _vpu_cycles(ROW_PIECE, D_MODEL, ops)
        rs = slice(r0, r0 + ROW_PIECE)
        y = _layer_norm(pre(rs), g_ref[...], b_ref[...])
        outs.append(y)
        outs_b.append((y if post is None else post(y)).astype(BF16))
    return jnp.concatenate(outs, axis=0), jnp.concatenate(outs_b, axis=0)


def _dense_tail(x_ref, p_ref, y_ref, rows, mix, ln1_g, ln1_b, w_up, w_down, ln2_g, ln2_b, w_proj, w_gate):
    n = rows.stop - rows.start

    def pre1(rs):
        return DEEPNORM_ALPHA * x_ref[rows.start + rs.start:rows.start + rs.stop, :] + mix[rs]

    x1, x1b = yield from _norm_stage(pre1, ln1_g, ln1_b, n)
    hid = []
    for n0 in range(0, D_FF, COL_PIECE):
        yield _mxu_cycles(n, D_MODEL, COL_PIECE), _vpu_cycles(n, COL_PIECE, 5)
        h = _dot(x1b, w_up[:, n0:n0 + COL_PIECE])
        hid.append(jnp.square(jnp.maximum(h, 0.0)).astype(BF16))
    down = yield from _matmul_stage(jnp.concatenate(hid, axis=1), w_down)

    def pre2(rs):
        return DEEPNORM_ALPHA * x1[rs] + down[rs]

    x2, x2b = yield from _norm_stage(pre2, ln2_g, ln2_b, n)
    for n0 in range(0, D_MODEL, COL_PIECE):
        cs = slice(n0, n0 + COL_PIECE)
        yield (_mxu_cycles(n, D_MODEL, COL_PIECE) + _mxu_cycles(n, D_PLE, COL_PIECE),
               _vpu_cycles(n, COL_PIECE, 7))
        gate = jax.nn.sigmoid(_dot(x2b, w_gate[:, cs]))
        y_ref[rows, cs] = x2[:, cs] + _mm(p_ref[rows, :], w_proj[:, cs]) * gate


def _gmlp_block(x_ref, rows, pos0, hist, keep, causal, w_in, b_in, ln_g, ln_b, b_s, w_out):
    n = rows.stop - rows.start
    z, xb = [], None
    for n0 in range(0, 2 * D_MODEL, COL_PIECE):
        yield _mxu_cycles(n, D_MODEL, COL_PIECE), _vpu_cycles(n, COL_PIECE, 11)
        if xb is None:
            xb = x_ref[rows, :].astype(BF16)
        cs = slice(n0, n0 + COL_PIECE)
        z.append(jax.nn.gelu(_dot(xb, w_in[:, cs]) + b_in[:, cs], approximate=True))
    z = jnp.concatenate(z, axis=1)
    u, v = z[:, :D_MODEL], z[:, D_MODEL:]
    vn, vnb = yield from _norm_stage(lambda rs: v[rs], ln_g, ln_b, n, ops=11)
    gated = []
    for c in range(n // CHUNK):
        rs = slice(c * CHUNK, (c + 1) * CHUNK)
        heads = []
        for h in range(A_HEADS):
            if h % 2 == 0:
                yield 2 * _mxu_cycles(CHUNK, A_GROUP, A_GROUP), 2 * _vpu_cycles(CHUNK, A_GROUP, 4)
            cs = slice(h * A_GROUP, (h + 1) * A_GROUP)
            s = _dot(causal[h], vnb[rs, cs]) + b_s[h]
            heads.append((u[rs, cs] * s).astype(BF16))
        gated.append(jnp.concatenate(heads, axis=1))
    mix = yield from _matmul_stage(jnp.concatenate(gated, axis=0), w_out)
    return mix, vn[n - CHUNK:, :]


def _pool_block(x_ref, rows, pos0, hist, keep, w_in, w_grp, scale, w_out):
    n = rows.stop - rows.start
    h, xb = [], None
    for n0 in range(0, D_MODEL, COL_PIECE):
        yield _mxu_cycles(n, D_MODEL, COL_PIECE), 0
        if xb is None:
            xb = x_ref[rows, :].astype(BF16)
        h.append(_dot(xb, w_in[:, n0:n0 + COL_PIECE]))
    h = jnp.concatenate(h, axis=1)
    pos1 = (pos0 + 1 + lax.broadcasted_iota(jnp.int32, (n, 1), 0)).astype(F32)
    mixed = []
    for g, w in enumerate(POOL_WINDOWS):
        assert w & (w - 1) == 0 and w <= POOL_HALO
        yield _mxu_cycles(n, B_GROUP, B_GROUP), _vpu_cycles(n, B_GROUP, 4 * w.bit_length() + 4)
        cs = slice(g * B_GROUP, (g + 1) * B_GROUP)
        hg = h[:, cs]
        keep(cs, hg[n - POOL_HALO:, :])
        acc = jnp.concatenate([hist(cs), hg], axis=0)
        step = 1
        while step < w:
            acc = acc + pltpu.roll(acc, step, axis=0)
            step *= 2
        d = acc[POOL_HALO:, :] / jnp.minimum(float(w), pos1) - hg
        mixed.append((_mm(d, w_grp[g]) * scale[:, cs]).astype(BF16))
    mix = yield from _matmul_stage(jnp.concatenate(mixed, axis=1), w_out)
    return mix, None


def _conformer_block(x_ref, rows, pos0, hist, keep, w_in, b_in, w_dw, b_dw, ln_g, ln_b, w_out):
    n = rows.stop - rows.start
    y, xb = [], None
    for n0 in range(0, D_MODEL, MIX_PIECE):
        yield 2 * _mxu_cycles(n, D_MODEL, MIX_PIECE), _vpu_cycles(n, MIX_PIECE, 7)
        if xb is None:
            xb = x_ref[rows, :].astype(BF16)
        cs = slice(n0, n0 + MIX_PIECE)
        cs2 = slice(D_MODEL + n0, D_MODEL + n0 + MIX_PIECE)
        a = _dot(xb, w_in[:, cs]) + b_in[:, cs]
        g = a * jax.nn.sigmoid(_dot(xb, w_in[:, cs2]) + b_in[:, cs2])
        keep(cs, g[n - CONV_HALO:, :])
        ext = jnp.concatenate([hist(cs), g], axis=0)
        acc = None
        for r in range(SUBLANES):
            taps = (CONV_K - 1 - r) // SUBLANES + 1
            yield 0, _vpu_cycles(n, MIX_PIECE, 2 * taps + 3)
            ext_r = ext if r == 0 else pltpu.roll(ext, r, axis=0)
            for a_ in range(taps):
                k = CONV_K - 1 - (SUBLANES * a_ + r)
                start = CONV_HALO - SUBLANES * a_
                term = ext_r[start:start + n, :] * w_dw[k:k + 1, cs]
                acc = term if acc is None else acc + term
        y.append(acc)
    y = jnp.concatenate(y, axis=1)
    _, act = yield from _norm_stage(lambda rs: y[rs] + b_dw[...], ln_g, ln_b, n,
                                    post=lambda yn: yn * jax.nn.sigmoid(yn), ops=16)
    mix = yield from _matmul_stage(act, w_out)
    return mix, None


def _sconv_block(x_ref, rows, pos0, hist, keep, w_in, w_conv, w_out):
    n = rows.stop - rows.start
    gated, xb = [], None
    for n0 in range(0, D_MODEL, MIX_PIECE):
        yield 3 * _mxu_cycles(n, D_MODEL, MIX_PIECE), _vpu_cycles(n, MIX_PIECE, 14)
        if xb is None:
            xb = x_ref[rows, :].astype(BF16)
        cs = slice(n0, n0 + MIX_PIECE)
        bg, cg, hh = (_dot(xb, w_in[:, i * D_MODEL + n0:i * D_MODEL + n0 + MIX_PIECE]) for i in range(3))
        q = cg * hh
        keep(cs, q[n - SCONV_HALO:, :])
        ext = jnp.concatenate([hist(cs), q], axis=0)
        y = q * w_conv[SCONV_K - 1:SCONV_K, cs]
        for s in range(1, SCONV_K):
            k = SCONV_K - 1 - s
            y = y + pltpu.roll(ext, s, axis=0)[SCONV_HALO:, :] * w_conv[k:k + 1, cs]
        gated.append((bg * y).astype(BF16))
    mix = yield from _matmul_stage(jnp.concatenate(gated, axis=1), w_out)
    return mix, None


def _prompt_kernel(block_fn, n_mixer, sub, x_ref, p_ref, *rest):
    mixer, dense = rest[:n_mixer], rest[n_mixer:n_mixer + N_DENSE]
    y_ref, aux_ref, *scratch = rest[n_mixer + N_DENSE:]
    tile = x_ref.shape[0]
    n_blocks = tile // sub
    halo = scratch[0] if scratch else None
    if halo is not None:
        @pl.when(pl.program_id(1) == 0)
        def _():
            halo[...] = jnp.zeros_like(halo)

    if block_fn is _gmlp_block:
        w_s, mixer = mixer[0], mixer[1:]
        row = lax.broadcasted_iota(jnp.int32, (CHUNK, CHUNK), 0)
        col = lax.broadcasted_iota(jnp.int32, (CHUNK, CHUNK), 1)
        mixer = ([jnp.where(row >= col, w_s[h], 0.0).astype(BF16) for h in range(A_HEADS)],) + tuple(mixer)

    handoff = [dict() for _ in range(n_blocks + 1)]

    def block(k):
        rows = slice(k * sub, (k + 1) * sub)
        pos0 = pl.program_id(1) * tile + k * sub

        def hist(cs):
            return halo[:, cs] if k == 0 else handoff[k][cs.start]

        def keep(cs, value):
            if k == n_blocks - 1:
                halo[:, cs] = value
            else:
                handoff[k + 1][cs.start] = value

        mix, tail = yield from block_fn(x_ref, rows, pos0, hist, keep, *mixer)
        yield from _dense_tail(x_ref, p_ref, y_ref, rows, mix, *dense)
        return tail

    tails = _interleave([block(k) for k in range(n_blocks)])

    @pl.when(pl.program_id(1) == pl.num_programs(1) - 1)
    def _():
        if halo is not None:
            aux_ref[...] = halo[halo.shape[0] - aux_ref.shape[0]:, :]
        else:
            aux_ref[...] = tails[-1]


def _hist_row(st_ref, k, cols=slice(None)):
    return st_ref[:, k, cols]


def _sample_tail(x_ref, p_ref, y_ref, mix, dense):
    _run(_dense_tail(x_ref, p_ref, y_ref, slice(0, x_ref.shape[0]), mix, *dense))


def _sample_gmlp_kernel(x_ref, p_ref, w_in, b_in, ln_g, ln_b, w_s0, b_s0, w_out, *rest):
    dense, (y_ref, chunkv_ref) = rest[:N_DENSE], rest[N_DENSE:]
    z = jax.nn.gelu(_mm(x_ref[...], w_in[...]) + b_in[...], approximate=True)
    u = z[:, :D_MODEL]
    vn = _layer_norm(z[:, D_MODEL:], ln_g[...], ln_b[...])
    s = jnp.concatenate(
        [vn[:, h * A_GROUP:(h + 1) * A_GROUP] * w_s0[h] + b_s0[h] for h in range(A_HEADS)], axis=1)
    chunkv_ref[...] = vn
    _sample_tail(x_ref, p_ref, y_ref, _mm(u * s, w_out[...]), dense)


def _sample_pool_kernel(x_ref, p_ref, st_ref, w_in, w_grp, scale, w_out, *rest):
    dense, (y_ref, h_ref) = rest[:N_DENSE], rest[N_DENSE:]
    h = _mm(x_ref[...], w_in[...])
    mixed = []
    for g, w in enumerate(POOL_WINDOWS):
        cols = slice(g * B_GROUP, (g + 1) * B_GROUP)
        acc = h[:, cols]
        for k in range(POOL_BUF - (w - 1), POOL_BUF):
            acc = acc + _hist_row(st_ref, k, cols)
        d = acc / float(min(w, PAST_LEN + 1)) - h[:, cols]
        mixed.append(_mm(d, w_grp[g]))
    h_ref[...] = h
    mix = _mm(jnp.concatenate(mixed, axis=1) * scale[...], w_out[...])
    _sample_tail(x_ref, p_ref, y_ref, mix, dense)


def _sample_conformer_kernel(x_ref, p_ref, st_ref, w_in, b_in, w_dw, b_dw, ln_g, ln_b, w_out, *rest):
    dense, (y_ref, g_ref) = rest[:N_DENSE], rest[N_DENSE:]
    z = _mm(x_ref[...], w_in[...]) + b_in[...]
    g = z[:, :D_MODEL] * jax.nn.sigmoid(z[:, D_MODEL:])
    y = g * w_dw[CONV_K - 1:CONV_K, :]
    for k in range(CONV_K - 1):
        y = y + _hist_row(st_ref, k) * w_dw[k:k + 1, :]
    g_ref[...] = g
    yn = _layer_norm(y + b_dw[...], ln_g[...], ln_b[...])
    mix = _mm(yn * jax.nn.sigmoid(yn), w_out[...])
    _sample_tail(x_ref, p_ref, y_ref, mix, dense)


def _sample_sconv_kernel(x_ref, p_ref, st_ref, w_in, w_conv, w_out, *rest):
    dense, (y_ref, q_ref) = rest[:N_DENSE], rest[N_DENSE:]
    z = _mm(x_ref[...], w_in[...])
    bg = z[:, :D_MODEL]
    q = z[:, D_MODEL:2 * D_MODEL] * z[:, 2 * D_MODEL:]
    y = q * w_conv[SCONV_K - 1:SCONV_K, :]
    for k in range(SCONV_K - 1):
        y = y + _hist_row(st_ref, k) * w_conv[k:k + 1, :]
    q_ref[...] = q
    _sample_tail(x_ref, p_ref, y_ref, _mm(bg * y, w_out[...]), dense)


def _resident(arr, lead=None):
    if lead is None:
        shape, idx = arr.shape, (0,) * arr.ndim
    else:
        shape, idx = (None,) + arr.shape[1:], (lead,) + (0,) * (arr.ndim - 1)
    return pl.BlockSpec(shape, lambda *_: idx, pipeline_mode=pl.Buffered(1))


def _smem():
    return pl.BlockSpec(memory_space=pltpu.SMEM)


def _dense_operands(layer, prm):
    arrs = [prm['ln1_g'], prm['ln1_b'], prm['mlp_w_up'], prm['mlp_w_down'],
            prm['ln2_g'], prm['ln2_b'], prm['ple_w_proj'], prm['ple_w_gate']]
    assert len(arrs) == N_DENSE
    return arrs, [_resident(a, lead=layer) for a in arrs]


def _prompt_layer(layer, x, p, prm):
    batch, seq, _ = x.shape
    tile, sub = PROMPT_TILES[layer % 4], PROMPT_SUB
    assert seq % tile == 0 and tile % sub == 0 and sub % CHUNK == 0 and sub >= CONV_HALO
    kind = layer % 4
    if kind == 0:
        block_fn, aux_rows, halo_rows = _gmlp_block, CHUNK, None
        mixer = [prm['a_w_s'], prm['a_w_in'], prm['a_b_in'], prm['a_ln_g'], prm['a_ln_b'],
                 prm['a_b_s3'], prm['a_w_out']]
    elif kind == 1:
        block_fn, aux_rows, halo_rows = _pool_block, POOL_BUF, POOL_HALO
        mixer = [prm['b_w_in'], prm['b_w_grp'], prm['b_scale'], prm['b_w_out']]
    elif kind == 2:
        block_fn, aux_rows, halo_rows = _conformer_block, CONV_K - 1, CONV_HALO
        mixer = [prm['c_w_in'], prm['c_b_in'], prm['c_w_dw'], prm['c_b_dw'],
                 prm['c_ln_g'], prm['c_ln_b'], prm['c_w_out']]
    else:
        block_fn, aux_rows, halo_rows = _sconv_block, SCONV_K - 1, SCONV_HALO
        mixer = [prm['d_w_in'], prm['d_w_conv'], prm['d_w_out']]
    dense, dense_specs = _dense_operands(layer, prm)
    in_specs = [
        pl.BlockSpec((None, tile, D_MODEL), lambda b, t: (b, t, 0)),
        pl.BlockSpec((None, None, tile, D_PLE), lambda b, t: (layer, b, t, 0)),
    ] + [_resident(a) for a in mixer] + dense_specs
    out_shape = (jax.ShapeDtypeStruct(x.shape, F32),
                 jax.ShapeDtypeStruct((batch, aux_rows, D_MODEL), F32))
    out_specs = (pl.BlockSpec((None, tile, D_MODEL), lambda b, t: (b, t, 0)),
                 pl.BlockSpec((None, aux_rows, D_MODEL), lambda b, t: (b, 0, 0)))
    scratch = [] if halo_rows is None else [pltpu.VMEM((halo_rows, D_MODEL), F32)]
    return pl.pallas_call(
        functools.partial(_prompt_kernel, block_fn, len(mixer), sub),
        out_shape=out_shape,
        grid=(batch, seq // tile),
        in_specs=in_specs,
        out_specs=out_specs,
        scratch_shapes=scratch,
        compiler_params=pltpu.CompilerParams(
            dimension_semantics=("arbitrary", "arbitrary"),
            vmem_limit_bytes=VMEM_LIMIT_BYTES),
        name=f"prompt_layer{layer}",
    )(x, p, *mixer, *dense)


def _sample_layer(layer, x, p, state, prm):
    rows = x.shape[0]
    kind = layer % 4
    if kind == 0:
        body = _sample_gmlp_kernel
        mixer = [prm['a_w_in'], prm['a_b_in'], prm['a_ln_g'], prm['a_ln_b'],
                 prm['a_w_s00'], prm['a_b_s0'], prm['a_w_out']]
        mixer_specs = [_resident(a) for a in mixer[:4]] + [_smem(), _smem(), _resident(mixer[6])]
    else:
        if kind == 1:
            body = _sample_pool_kernel
            mixer = [state, prm['b_w_in'], prm['b_w_grp'], prm['b_scale'], prm['b_w_out']]
        elif kind == 2:
            body = _sample_conformer_kernel
            mixer = [state, prm['c_w_in'], prm['c_b_in'], prm['c_w_dw'], prm['c_b_dw'],
                     prm['c_ln_g'], prm['c_ln_b'], prm['c_w_out']]
        else:
            body = _sample_sconv_kernel
            mixer = [state, prm['d_w_in'], prm['d_w_conv'], prm['d_w_out']]
        mixer_specs = [_resident(a) for a in mixer]
    dense, dense_specs = _dense_operands(layer, prm)
    in_specs = [_resident(x), _resident(p, lead=layer)] + mixer_specs + dense_specs
    out_shape = (jax.ShapeDtypeStruct((rows, D_MODEL), F32),) * 2
    out_specs = (pl.BlockSpec((rows, D_MODEL), lambda i: (0, 0)),) * 2
    return pl.pallas_call(
        body,
        out_shape=out_shape,
        grid=(1,),
        in_specs=in_specs,
        out_specs=out_specs,
        compiler_params=pltpu.CompilerParams(
            dimension_semantics=("arbitrary",),
            vmem_limit_bytes=VMEM_LIMIT_BYTES),
        name=f"sample_layer{layer}",
    )(x, p, *mixer, *dense)


def _row(v):
    return v.reshape(1, -1)


def kernel(x_prompt, x_sample, state_pool, state_conv, state_shortconv, p_prompt, p_sample, a_w_in, a_b_in, a_ln_g, a_ln_b, a_w_s, a_b_s, a_w_out, b_w_in, b_w_grp, b_scale, b_w_out, c_w_in, c_b_in, c_w_dw, c_b_dw, c_ln_g, c_ln_b, c_w_out, d_w_in, d_w_conv, d_w_out, ln1_g, ln1_b, ln2_g, ln2_b, mlp_w_up, mlp_w_down, ple_w_proj, ple_w_gate):
    rows = x_sample.shape[0]
    depth = mlp_w_up.shape[0]
    prm = {
        'a_w_in': a_w_in.astype(BF16), 'a_w_out': a_w_out.astype(BF16),
        'b_w_in': b_w_in.astype(BF16), 'b_w_grp': b_w_grp.astype(BF16), 'b_w_out': b_w_out.astype(BF16),
        'c_w_in': c_w_in.astype(BF16), 'c_w_out': c_w_out.astype(BF16),
        'd_w_in': d_w_in.astype(BF16), 'd_w_out': d_w_out.astype(BF16),
        'mlp_w_up': mlp_w_up.astype(BF16), 'mlp_w_down': mlp_w_down.astype(BF16),
        'ple_w_proj': ple_w_proj.astype(BF16), 'ple_w_gate': ple_w_gate.astype(BF16),
        'a_b_in': _row(a_b_in), 'a_ln_g': _row(a_ln_g), 'a_ln_b': _row(a_ln_b),
        'a_w_s': a_w_s, 'a_b_s3': a_b_s[:, :, None],
        'a_w_s00': a_w_s[:, 0, 0], 'a_b_s0': a_b_s[:, 0],
        'b_scale': _row(b_scale),
        'c_b_in': _row(c_b_in), 'c_w_dw': c_w_dw, 'c_b_dw': _row(c_b_dw),
        'c_ln_g': _row(c_ln_g), 'c_ln_b': _row(c_ln_b),
        'd_w_conv': d_w_conv,
        'ln1_g': ln1_g[:, None, :], 'ln1_b': ln1_b[:, None, :],
        'ln2_g': ln2_g[:, None, :], 'ln2_b': ln2_b[:, None, :],
    }

    aux_prompt = []
    x = x_prompt
    for layer in range(depth):
        x, aux = _prompt_layer(layer, x, p_prompt, prm)
        aux_prompt.append(aux)
    y_prompt = x

    states = [None, state_pool, state_conv, state_shortconv]
    p_s = p_sample.reshape(depth, rows, D_PLE)
    aux_sample = []
    x = x_sample.reshape(rows, D_MODEL)
    for layer in range(depth):
        x, aux = _sample_layer(layer, x, p_s, states[layer % 4], prm)
        aux_sample.append(aux[:, None, :])
    y_sample = x[:, None, :]

    def shifted(state, new_row):
        return jnp.concatenate([state[:, 1:, :], new_row], axis=1)

    return (y_prompt, y_sample, aux_prompt[0], aux_sample[0],
            aux_prompt[1], shifted(state_pool, aux_sample[1]),
            aux_prompt[2], shifted(state_conv, aux_sample[2]),
            aux_prompt[3], shifted(state_shortconv, aux_sample[3]))
```

```python
import functools

import jax
import jax.numpy as jnp
from jax import lax
from jax.experimental import pallas as pl
from jax.experimental.pallas import tpu as pltpu

D_MODEL = 1024
D_PLE = 256
D_FF = 4 * D_MODEL
DEPTH = 4
PAST_LEN = 16384
LN_EPS = 1e-5
DEEPNORM_ALPHA = (2 * DEPTH) ** 0.25
CHUNK = 128
A_GROUP = 128
A_HEADS = D_MODEL // A_GROUP
POOL_WINDOWS = (2, 4, 8, 16)
B_GROUP = D_MODEL // len(POOL_WINDOWS)
POOL_BUF = max(POOL_WINDOWS) - 1
CONV_K = 31
SCONV_K = 3

SUBLANES = 8
MXU_DIM = 256
N_MXU = 2
V7X_VMEM_BYTES = 64 * 1024 * 1024
VMEM_LIMIT_BYTES = V7X_VMEM_BYTES - 3 * 1024 * 1024

PROMPT_TILE = 1024
PROMPT_SUB = 256
ROW_PIECE = 32
COL_PIECE = 512
MIX_PIECE = 256
POOL_HALO = 16
CONV_HALO = 32
SCONV_HALO = 8
N_DENSE = 8

F32 = jnp.float32
BF16 = jnp.bfloat16


def _dot(a, b):
    return jnp.dot(a, b, preferred_element_type=F32)


def _mm(a, w):
    return _dot(a.astype(BF16), w)


def _layer_norm(x, g, b):
    mu = jnp.mean(x, axis=-1, keepdims=True)
    xc = x - mu
    var = jnp.mean(xc * xc, axis=-1, keepdims=True)
    return xc * lax.rsqrt(var + LN_EPS) * g + b


def _mxu_cycles(rows, k, n):
    tiles = -(-k // MXU_DIM) * -(-n // MXU_DIM)
    return rows // 2 * tiles // N_MXU


def _vpu_cycles(rows, cols, ops_per_vreg):
    return rows * cols * ops_per_vreg // (SUBLANES * 128 * 4)


def _run(gen):
    try:
        while True:
            next(gen)
    except StopIteration as stop:
        return stop.value


def _interleave(gens):
    results = [None] * len(gens)
    pending = {i: next(gen) for i, gen in enumerate(gens)}
    done = [0] * len(gens)
    ready = [0] * len(gens)
    free = [0, 0]

    def starts(i):
        m, v = pending[i]
        t_m = max(ready[i], free[0]) if m > 0 else ready[i]
        t_v = max(t_m, free[1]) if v > 0 else t_m
        return t_m, t_v

    while pending:
        allowed = [i for i in pending if i == 0 or (i - 1) not in pending or done[i] < done[i - 1]]
        i = min(allowed, key=lambda k: (starts(k)[0 if pending[k][0] > 0 else 1], k))
        (m, v), (t_m, t_v) = pending[i], starts(i)
        if m > 0:
            free[0] = t_m + m
        if v > 0:
            free[1] = t_v + v
        ready[i] = max(t_m + m, t_v + v)
        done[i] += 1
        try:
            pending[i] = next(gens[i])
        except StopIteration as stop:
            results[i] = stop.value
            del pending[i]
    return results


def _matmul_stage(lhs, w_ref, col_piece=COL_PIECE):
    rows, k = lhs.shape
    tiles = []
    for n0 in range(0, w_ref.shape[1], col_piece):
        yield _mxu_cycles(rows, k, col_piece), 0
        tiles.append(_dot(lhs, w_ref[:, n0:n0 + col_piece]))
    return jnp.concatenate(tiles, axis=1)


def _norm_stage(pre, g_ref, b_ref, rows, post=None, ops=14):
    outs, outs_b = [], []
    for r0 in range(0, rows, ROW_PIECE):
        yield 0, _vpu_cycles(ROW_PIECE, D_MODEL, ops)
        rs = slice(r0, r0 + ROW_PIECE)
        y = _layer_norm(pre(rs), g_ref[...], b_ref[...])
        outs.append(y)
        outs_b.append((y if post is None else post(y)).astype(BF16))
    return jnp.concatenate(outs, axis=0), jnp.concatenate(outs_b, axis=0)


def _dense_tail(x_ref, p_ref, y_ref, rows, mix, ln1_g, ln1_b, w_up, w_down, ln2_g, ln2_b, w_proj, w_gate):
    n = rows.stop - rows.start

    def pre1(rs):
        return DEEPNORM_ALPHA * x_ref[rows.start + rs.start:rows.start + rs.stop, :] + mix[rs]

    x1, x1b = yield from _norm_stage(pre1, ln1_g, ln1_b, n)
    hid = []
    for n0 in range(0, D_FF, COL_PIECE):
        yield _mxu_cycles(n, D_MODEL, COL_PIECE), _vpu_cycles(n, COL_PIECE, 5)
        h = _dot(x1b, w_up[:, n0:n0 + COL_PIECE])
        hid.append(jnp.square(jnp.maximum(h, 0.0)).astype(BF16))
    down = yield from _matmul_stage(jnp.concatenate(hid, axis=1), w_down)

    def pre2(rs):
        return DEEPNORM_ALPHA * x1[rs] + down[rs]

    x2, x2b = yield from _norm_stage(pre2, ln2_g, ln2_b, n)
    for n0 in range(0, D_MODEL, COL_PIECE):
        cs = slice(n0, n0 + COL_PIECE)
        yield (_mxu_cycles(n, D_MODEL, COL_PIECE) + _mxu_cycles(n, D_PLE, COL_PIECE),
               _vpu_cycles(n, COL_PIECE, 7))
        gate = jax.nn.sigmoid(_dot(x2b, w_gate[:, cs]))
        y_ref[rows, cs] = x2[:, cs] + _mm(p_ref[rows, :], w_proj[:, cs]) * gate


def _gmlp_block(x_ref, rows, pos0, hist, keep, causal, w_in, b_in, ln_g, ln_b, b_s, w_out):
    n = rows.stop - rows.start
    z, xb = [], None
    for n0 in range(0, 2 * D_MODEL, COL_PIECE):
        yield _mxu_cycles(n, D_MODEL, COL_PIECE), _vpu_cycles(n, COL_PIECE, 11)
        if xb is None:
            xb = x_ref[rows, :].astype(BF16)
        cs = slice(n0, n0 + COL_PIECE)
        z.append(jax.nn.gelu(_dot(xb, w_in[:, cs]) + b_in[:, cs], approximate=True))
    z = jnp.concatenate(z, axis=1)
    u, v = z[:, :D_MODEL], z[:, D_MODEL:]
    vn, vnb = yield from _norm_stage(lambda rs: v[rs], ln_g, ln_b, n, ops=11)
    gated = []
    for c in range(n // CHUNK):
        rs = slice(c * CHUNK, (c + 1) * CHUNK)
        heads = []
        for h in range(A_HEADS):
            if h % 2 == 0:
                yield 2 * _mxu_cycles(CHUNK, A_GROUP, A_GROUP), 2 * _vpu_cycles(CHUNK, A_GROUP, 4)
            cs = slice(h * A_GROUP, (h + 1) * A_GROUP)
            s = _dot(causal[h], vnb[rs, cs]) + b_s[h]
            heads.append((u[rs, cs] * s).astype(BF16))
        gated.append(jnp.concatenate(heads, axis=1))
    mix = yield from _matmul_stage(jnp.concatenate(gated, axis=0), w_out)
    return mix, vn[n - CHUNK:, :]


def _pool_block(x_ref, rows, pos0, hist, keep, w_in, w_grp, scale, w_out):
    n = rows.stop - rows.start
    h, xb = [], None
    for n0 in range(0, D_MODEL, COL_PIECE):
        yield _mxu_cycles(n, D_MODEL, COL_PIECE), 0
        if xb is None:
            xb = x_ref[rows, :].astype(BF16)
        h.append(_dot(xb, w_in[:, n0:n0 + COL_PIECE]))
    h = jnp.concatenate(h, axis=1)
    pos1 = (pos0 + 1 + lax.broadcasted_iota(jnp.int32, (n, 1), 0)).astype(F32)
    mixed = []
    for g, w in enumerate(POOL_WINDOWS):
        assert w & (w - 1) == 0 and w <= POOL_HALO
        yield _mxu_cycles(n, B_GROUP, B_GROUP), _vpu_cycles(n, B_GROUP, 4 * w.bit_length() + 4)
        cs = slice(g * B_GROUP, (g + 1) * B_GROUP)
        hg = h[:, cs]
        keep(cs, hg[n - POOL_HALO:, :])
        acc = jnp.concatenate([hist(cs), hg], axis=0)
        step = 1
        while step < w:
            acc = acc + pltpu.roll(acc, step, axis=0)
            step *= 2
        d = acc[POOL_HALO:, :] / jnp.minimum(float(w), pos1) - hg
        mixed.append((_mm(d, w_grp[g]) * scale[:, cs]).astype(BF16))
    mix = yield from _matmul_stage(jnp.concatenate(mixed, axis=1), w_out)
    return mix, None


def _conformer_block(x_ref, rows, pos0, hist, keep, w_in, b_in, w_dw, b_dw, ln_g, ln_b, w_out):
    n = rows.stop - rows.start
    y, xb = [], None
    for n0 in range(0, D_MODEL, MIX_PIECE):
        yield 2 * _mxu_cycles(n, D_MODEL, MIX_PIECE), _vpu_cycles(n, MIX_PIECE, 7)
        if xb is None:
            xb = x_ref[rows, :].astype(BF16)
        cs = slice(n0, n0 + MIX_PIECE)
        cs2 = slice(D_MODEL + n0, D_MODEL + n0 + MIX_PIECE)
        a = _dot(xb, w_in[:, cs]) + b_in[:, cs]
        g = a * jax.nn.sigmoid(_dot(xb, w_in[:, cs2]) + b_in[:, cs2])
        keep(cs, g[n - CONV_HALO:, :])
        ext = jnp.concatenate([hist(cs), g], axis=0)
        acc = None
        for r in range(SUBLANES):
            taps = (CONV_K - 1 - r) // SUBLANES + 1
            yield 0, _vpu_cycles(n, MIX_PIECE, 2 * taps + 3)
            ext_r = ext if r == 0 else pltpu.roll(ext, r, axis=0)
            for a_ in range(taps):
                k = CONV_K - 1 - (SUBLANES * a_ + r)
                start = CONV_HALO - SUBLANES * a_
                term = ext_r[start:start + n, :] * w_dw[k:k + 1, cs]
                acc = term if acc is None else acc + term
        y.append(acc)
    y = jnp.concatenate(y, axis=1)
    _, act = yield from _norm_stage(lambda rs: y[rs] + b_dw[...], ln_g, ln_b, n,
                                    post=lambda yn: yn * jax.nn.sigmoid(yn), ops=16)
    mix = yield from _matmul_stage(act, w_out)
    return mix, None


def _sconv_block(x_ref, rows, pos0, hist, keep, w_in, w_conv, w_out):
    n = rows.stop - rows.start
    gated, xb = [], None
    for n0 in range(0, D_MODEL, MIX_PIECE):
        yield 3 * _mxu_cycles(n, D_MODEL, MIX_PIECE), _vpu_cycles(n, MIX_PIECE, 14)
        if xb is None:
            xb = x_ref[rows, :].astype(BF16)
        cs = slice(n0, n0 + MIX_PIECE)
        bg, cg, hh = (_dot(xb, w_in[:, i * D_MODEL + n0:i * D_MODEL + n0 + MIX_PIECE]) for i in range(3))
        q = cg * hh
        keep(cs, q[n - SCONV_HALO:, :])
        ext = jnp.concatenate([hist(cs), q], axis=0)
        y = q * w_conv[SCONV_K - 1:SCONV_K, cs]
        for s in range(1, SCONV_K):
            k = SCONV_K - 1 - s
            y = y + pltpu.roll(ext, s, axis=0)[SCONV_HALO:, :] * w_conv[k:k + 1, cs]
        gated.append((bg * y).astype(BF16))
    mix = yield from _matmul_stage(jnp.concatenate(gated, axis=1), w_out)
    return mix, None


def _prompt_kernel(block_fn, n_mixer, sub, x_ref, p_ref, *rest):
    mixer, dense = rest[:n_mixer], rest[n_mixer:n_mixer + N_DENSE]
    y_ref, aux_ref, *scratch = rest[n_mixer + N_DENSE:]
    tile = x_ref.shape[0]
    n_blocks = tile // sub
    halo = scratch[0] if scratch else None
    if halo is not None:
        @pl.when(pl.program_id(1) == 0)
        def _():
            halo[...] = jnp.zeros_like(halo)

    if block_fn is _gmlp_block:
        w_s, mixer = mixer[0], mixer[1:]
        row = lax.broadcasted_iota(jnp.int32, (CHUNK, CHUNK), 0)
        col = lax.broadcasted_iota(jnp.int32, (CHUNK, CHUNK), 1)
        mixer = ([jnp.where(row >= col, w_s[h], 0.0).astype(BF16) for h in range(A_HEADS)],) + tuple(mixer)

    handoff = [dict() for _ in range(n_blocks + 1)]

    def block(k):
        rows = slice(k * sub, (k + 1) * sub)
        pos0 = pl.program_id(1) * tile + k * sub

        def hist(cs):
            return halo[:, cs] if k == 0 else handoff[k][cs.start]

        def keep(cs, value):
            if k == n_blocks - 1:
                halo[:, cs] = value
            else:
                handoff[k + 1][cs.start] = value

        mix, tail = yield from block_fn(x_ref, rows, pos0, hist, keep, *mixer)
        yield from _dense_tail(x_ref, p_ref, y_ref, rows, mix, *dense)
        return tail

    tails = _interleave([block(k) for k in range(n_blocks)])

    @pl.when(pl.program_id(1) == pl.num_programs(1) - 1)
    def _():
        if halo is not None:
            aux_ref[...] = halo[halo.shape[0] - aux_ref.shape[0]:, :]
        else:
            aux_ref[...] = tails[-1]


def _hist_row(st_ref, k, cols=slice(None)):
    return st_ref[:, k, cols]


def _sample_tail(x_ref, p_ref, y_ref, mix, dense):
    _run(_dense_tail(x_ref, p_ref, y_ref, slice(0, x_ref.shape[0]), mix, *dense))


def _sample_gmlp_kernel(x_ref, p_ref, w_in, b_in, ln_g, ln_b, w_s0, b_s0, w_out, *rest):
    dense, (y_ref, chunkv_ref) = rest[:N_DENSE], rest[N_DENSE:]
    z = jax.nn.gelu(_mm(x_ref[...], w_in[...]) + b_in[...], approximate=True)
    u = z[:, :D_MODEL]
    vn = _layer_norm(z[:, D_MODEL:], ln_g[...], ln_b[...])
    s = jnp.concatenate(
        [vn[:, h * A_GROUP:(h + 1) * A_GROUP] * w_s0[h] + b_s0[h] for h in range(A_HEADS)], axis=1)
    chunkv_ref[...] = vn
    _sample_tail(x_ref, p_ref, y_ref, _mm(u * s, w_out[...]), dense)


def _sample_pool_kernel(x_ref, p_ref, st_ref, w_in, w_grp, scale, w_out, *rest):
    dense, (y_ref, h_ref) = rest[:N_DENSE], rest[N_DENSE:]
    h = _mm(x_ref[...], w_in[...])
    mixed = []
    for g, w in enumerate(POOL_WINDOWS):
        cols = slice(g * B_GROUP, (g + 1) * B_GROUP)
        acc = h[:, cols]
        for k in range(POOL_BUF - (w - 1), POOL_BUF):
            acc = acc + _hist_row(st_ref, k, cols)
        d = acc / float(min(w, PAST_LEN + 1)) - h[:, cols]
        mixed.append(_mm(d, w_grp[g]))
    h_ref[...] = h
    mix = _mm(jnp.concatenate(mixed, axis=1) * scale[...], w_out[...])
    _sample_tail(x_ref, p_ref, y_ref, mix, dense)


def _sample_conformer_kernel(x_ref, p_ref, st_ref, w_in, b_in, w_dw, b_dw, ln_g, ln_b, w_out, *rest):
    dense, (y_ref, g_ref) = rest[:N_DENSE], rest[N_DENSE:]
    z = _mm(x_ref[...], w_in[...]) + b_in[...]
    g = z[:, :D_MODEL] * jax.nn.sigmoid(z[:, D_MODEL:])
    y = g * w_dw[CONV_K - 1:CONV_K, :]
    for k in range(CONV_K - 1):
        y = y + _hist_row(st_ref, k) * w_dw[k:k + 1, :]
    g_ref[...] = g
    yn = _layer_norm(y + b_dw[...], ln_g[...], ln_b[...])
    mix = _mm(yn * jax.nn.sigmoid(yn), w_out[...])
    _sample_tail(x_ref, p_ref, y_ref, mix, dense)


def _sample_sconv_kernel(x_ref, p_ref, st_ref, w_in, w_conv, w_out, *rest):
    dense, (y_ref, q_ref) = rest[:N_DENSE], rest[N_DENSE:]
    z = _mm(x_ref[...], w_in[...])
    bg = z[:, :D_MODEL]
    q = z[:, D_MODEL:2 * D_MODEL] * z[:, 2 * D_MODEL:]
    y = q * w_conv[SCONV_K - 1:SCONV_K, :]
    for k in range(SCONV_K - 1):
        y = y + _hist_row(st_ref, k) * w_conv[k:k + 1, :]
    q_ref[...] = q
    _sample_tail(x_ref, p_ref, y_ref, _mm(bg * y, w_out[...]), dense)


def _resident(arr, lead=None):
    if lead is None:
        shape, idx = arr.shape, (0,) * arr.ndim
    else:
        shape, idx = (None,) + arr.shape[1:], (lead,) + (0,) * (arr.ndim - 1)
    return pl.BlockSpec(shape, lambda *_: idx, pipeline_mode=pl.Buffered(1))


def _smem():
    return pl.BlockSpec(memory_space=pltpu.SMEM)


def _fusible(*operands):
    return [a.dtype == BF16 for a in operands]


def _dense_operands(layer, prm):
    arrs = [prm['ln1_g'], prm['ln1_b'], prm['mlp_w_up'], prm['mlp_w_down'],
            prm['ln2_g'], prm['ln2_b'], prm['ple_w_proj'], prm['ple_w_gate']]
    assert len(arrs) == N_DENSE
    return arrs, [_resident(a, lead=layer) for a in arrs]


def _prompt_layer(layer, x, p, prm):
    batch, seq, _ = x.shape
    tile, sub = PROMPT_TILE, PROMPT_SUB
    assert seq % tile == 0 and tile % sub == 0 and sub % CHUNK == 0 and sub >= CONV_HALO
    kind = layer % 4
    if kind == 0:
        block_fn, aux_rows, halo_rows = _gmlp_block, CHUNK, None
        mixer = [prm['a_w_s'], prm['a_w_in'], prm['a_b_in'], prm['a_ln_g'], prm['a_ln_b'],
                 prm['a_b_s3'], prm['a_w_out']]
    elif kind == 1:
        block_fn, aux_rows, halo_rows = _pool_block, POOL_BUF, POOL_HALO
        mixer = [prm['b_w_in'], prm['b_w_grp'], prm['b_scale'], prm['b_w_out']]
    elif kind == 2:
        block_fn, aux_rows, halo_rows = _conformer_block, CONV_K - 1, CONV_HALO
        mixer = [prm['c_w_in'], prm['c_b_in'], prm['c_w_dw'], prm['c_b_dw'],
                 prm['c_ln_g'], prm['c_ln_b'], prm['c_w_out']]
    else:
        block_fn, aux_rows, halo_rows = _sconv_block, SCONV_K - 1, SCONV_HALO
        mixer = [prm['d_w_in'], prm['d_w_conv'], prm['d_w_out']]
    dense, dense_specs = _dense_operands(layer, prm)
    in_specs = [
        pl.BlockSpec((None, tile, D_MODEL), lambda b, t: (b, t, 0)),
        pl.BlockSpec((None, None, tile, D_PLE), lambda b, t: (layer, b, t, 0)),
    ] + [_resident(a) for a in mixer] + dense_specs
    out_shape = (jax.ShapeDtypeStruct(x.shape, F32),
                 jax.ShapeDtypeStruct((batch, aux_rows, D_MODEL), F32))
    out_specs = (pl.BlockSpec((None, tile, D_MODEL), lambda b, t: (b, t, 0)),
                 pl.BlockSpec((None, aux_rows, D_MODEL), lambda b, t: (b, 0, 0)))
    scratch = [] if halo_rows is None else [pltpu.VMEM((halo_rows, D_MODEL), F32)]
    return pl.pallas_call(
        functools.partial(_prompt_kernel, block_fn, len(mixer), sub),
        out_shape=out_shape,
        grid=(batch, seq // tile),
        in_specs=in_specs,
        out_specs=out_specs,
        scratch_shapes=scratch,
        compiler_params=pltpu.CompilerParams(
            dimension_semantics=("arbitrary", "arbitrary"),
            allow_input_fusion=_fusible(x, p, *mixer, *dense),
            vmem_limit_bytes=VMEM_LIMIT_BYTES),
        name=f"prompt_layer{layer}",
    )(x, p, *mixer, *dense)


def _sample_layer(layer, x, p, state, prm):
    rows = x.shape[0]
    kind = layer % 4
    if kind == 0:
        body = _sample_gmlp_kernel
        mixer = [prm['a_w_in'], prm['a_b_in'], prm['a_ln_g'], prm['a_ln_b'],
                 prm['a_w_s00'], prm['a_b_s0'], prm['a_w_out']]
        mixer_specs = [_resident(a) for a in mixer[:4]] + [_smem(), _smem(), _resident(mixer[6])]
    else:
        if kind == 1:
            body = _sample_pool_kernel
            mixer = [state, prm['b_w_in'], prm['b_w_grp'], prm['b_scale'], prm['b_w_out']]
        elif kind == 2:
            body = _sample_conformer_kernel
            mixer = [state, prm['c_w_in'], prm['c_b_in'], prm['c_w_dw'], prm['c_b_dw'],
                     prm['c_ln_g'], prm['c_ln_b'], prm['c_w_out']]
        else:
            body = _sample_sconv_kernel
            mixer = [state, prm['d_w_in'], prm['d_w_conv'], prm['d_w_out']]
        mixer_specs = [_resident(a) for a in mixer]
    dense, dense_specs = _dense_operands(layer, prm)
    in_specs = [_resident(x), _resident(p, lead=layer)] + mixer_specs + dense_specs
    out_shape = (jax.ShapeDtypeStruct((rows, D_MODEL), F32),) * 2
    out_specs = (pl.BlockSpec((rows, D_MODEL), lambda i: (0, 0)),) * 2
    return pl.pallas_call(
        body,
        out_shape=out_shape,
        grid=(1,),
        in_specs=in_specs,
        out_specs=out_specs,
        compiler_params=pltpu.CompilerParams(
            dimension_semantics=("arbitrary",),
            allow_input_fusion=_fusible(x, p, *mixer, *dense),
            vmem_limit_bytes=VMEM_LIMIT_BYTES),
        name=f"sample_layer{layer}",
    )(x, p, *mixer, *dense)


def _row(v):
    return v.reshape(1, -1)


def kernel(x_prompt, x_sample, state_pool, state_conv, state_shortconv, p_prompt, p_sample, a_w_in, a_b_in, a_ln_g, a_ln_b, a_w_s, a_b_s, a_w_out, b_w_in, b_w_grp, b_scale, b_w_out, c_w_in, c_b_in, c_w_dw, c_b_dw, c_ln_g, c_ln_b, c_w_out, d_w_in, d_w_conv, d_w_out, ln1_g, ln1_b, ln2_g, ln2_b, mlp_w_up, mlp_w_down, ple_w_proj, ple_w_gate):
    rows = x_sample.shape[0]
    depth = mlp_w_up.shape[0]
    prm = {
        'a_w_in': a_w_in.astype(BF16), 'a_w_out': a_w_out.astype(BF16),
        'b_w_in': b_w_in.astype(BF16), 'b_w_grp': b_w_grp.astype(BF16), 'b_w_out': b_w_out.astype(BF16),
        'c_w_in': c_w_in.astype(BF16), 'c_w_out': c_w_out.astype(BF16),
        'd_w_in': d_w_in.astype(BF16), 'd_w_out': d_w_out.astype(BF16),
        'mlp_w_up': mlp_w_up.astype(BF16), 'mlp_w_down': mlp_w_down.astype(BF16),
        'ple_w_proj': ple_w_proj.astype(BF16), 'ple_w_gate': ple_w_gate.astype(BF16),
        'a_b_in': _row(a_b_in), 'a_ln_g': _row(a_ln_g), 'a_ln_b': _row(a_ln_b),
        'a_w_s': a_w_s, 'a_b_s3': a_b_s[:, :, None],
        'a_w_s00': a_w_s[:, 0, 0], 'a_b_s0': a_b_s[:, 0],
        'b_scale': _row(b_scale),
        'c_b_in': _row(c_b_in), 'c_w_dw': c_w_dw, 'c_b_dw': _row(c_b_dw),
        'c_ln_g': _row(c_ln_g), 'c_ln_b': _row(c_ln_b),
        'd_w_conv': d_w_conv,
        'ln1_g': ln1_g[:, None, :], 'ln1_b': ln1_b[:, None, :],
        'ln2_g': ln2_g[:, None, :], 'ln2_b': ln2_b[:, None, :],
    }

    aux_prompt = []
    x = x_prompt
    for layer in range(depth):
        x, aux = _prompt_layer(layer, x, p_prompt, prm)
        aux_prompt.append(aux)
    y_prompt = x

    states = [None, state_pool, state_conv, state_shortconv]
    p_s = p_sample.reshape(depth, rows, D_PLE)
    aux_sample = []
    x = x_sample.reshape(rows, D_MODEL)
    for layer in range(depth):
        x, aux = _sample_layer(layer, x, p_s, states[layer % 4], prm)
        aux_sample.append(aux[:, None, :])
    y_sample = x[:, None, :]

    def shifted(state, new_row):
        return jnp.concatenate([state[:, 1:, :], new_row], axis=1)

    return (y_prompt, y_sample, aux_prompt[0], aux_sample[0],
            aux_prompt[1], shifted(state_pool, aux_sample[1]),
            aux_prompt[2], shifted(state_conv, aux_sample[2]),
            aux_prompt[3], shifted(state_shortconv, aux_sample[3]))
```
